```python
import math
import jax, jax.numpy as jnp
from jax import lax
import numpy as np


D_MODEL = 4096
BATCH = 4
SEQ = 2048
DEPTH = 1
DEC_BATCH = 128
DEC_SEQ = 4
PAST_LEN = 2048
PAGE_SIZE = 128

DA_WIDTH = D_MODEL // 2
GLA_WIDTH = D_MODEL - DA_WIDTH
DA_HD = 128
DA_VD = 2 * DA_HD
DA_HEADS = DA_WIDTH // DA_VD
ROT_DIM = DA_HD // 4
ROPE_THETA = 500000.0
Q_BLOCK = 128
GLA_HEADS = 4
GLA_DV = GLA_WIDTH // GLA_HEADS
GLA_DK = GLA_DV // 2
GATE_RANK = 16
GATE_TAU = 16.0
GLA_CHUNK = 64
D_FF = ((8 * D_MODEL // 3 + 255) // 256) * 256
EPS = 1e-6
IN_SIZES = (DA_HEADS * 2 * DA_HD, DA_HEADS * 2 * DA_HD, DA_HEADS * DA_VD,
            GLA_HEADS * GLA_DK, GLA_HEADS * GLA_DK, GLA_HEADS * GLA_DV, GLA_WIDTH, GATE_RANK)

kernel_name = 'hymba_diffattn_gla_macaron_step'


def lambda_init(layer):
    return 0.8 - 0.6 * math.exp(-0.3 * layer)


def rmsnorm(x, g):
    xf = x.astype(jnp.float32)
    y = xf * lax.rsqrt(jnp.mean(xf * xf, axis=-1, keepdims=True) + EPS)
    return (y * g.astype(jnp.float32)).astype(x.dtype)


def swiglu(x, w_gate, w_up, w_down):
    return (jax.nn.silu(x @ w_gate) * (x @ w_up)) @ w_down


def partial_rope(x, pos):
    half = ROT_DIM // 2
    inv = ROPE_THETA ** (-jnp.arange(half, dtype=jnp.float32) * 2.0 / ROT_DIM)
    ang = pos.astype(jnp.float32)[:, None] * inv[None, :]
    cos = jnp.cos(ang)[None, :, None, None, :]
    sin = jnp.sin(ang)[None, :, None, None, :]
    xf = x.astype(jnp.float32)
    x1 = xf[..., :half]
    x2 = xf[..., half:ROT_DIM]
    out = jnp.concatenate([x1 * cos - x2 * sin, x2 * cos + x1 * sin, xf[..., ROT_DIM:]], axis=-1)
    return out.astype(x.dtype)


def diff_attend(q, k, v, mask, lam):
    s = jnp.einsum('bqhcd,bkhcd->bhcqk', q, k).astype(jnp.float32) * (DA_HD ** -0.5)
    s = jnp.where(mask, s, -jnp.inf)
    p = jax.nn.softmax(s, axis=-1)
    w = p[:, :, 0] - lam * p[:, :, 1]
    return jnp.einsum('bhqk,bkhe->bqhe', w.astype(v.dtype), v)


def prompt_attention(q, k, v, lam):
    B, L = q.shape[:2]
    nb = L // Q_BLOCK
    qb = q.reshape(B, nb, Q_BLOCK, DA_HEADS, 2, DA_HD).swapaxes(0, 1)
    kpos = jnp.arange(L)

    def one(args):
        qi, i = args
        qpos = i * Q_BLOCK + jnp.arange(Q_BLOCK)
        return diff_attend(qi, k, v, qpos[:, None] >= kpos[None, :], lam)

    ob = lax.map(one, (qb, jnp.arange(nb)))
    return ob.swapaxes(0, 1).reshape(B, L, DA_HEADS, DA_VD)


def sample_attention(q, k, v, lam, cache_k, cache_v, page_table, layer):
    Ld = q.shape[1]
    mask = jnp.concatenate([jnp.ones((Ld, PAST_LEN), bool), jnp.tril(jnp.ones((Ld, Ld), bool))], axis=1)

    def one(args):
        qi, ki, vi, pt = args
        kp = cache_k[layer, pt].reshape(PAST_LEN, DA_HEADS, 2, DA_HD)
        vp = cache_v[layer, pt].reshape(PAST_LEN, DA_HEADS, DA_VD)
        kk = jnp.concatenate([kp, ki], axis=0)[None]
        vv = jnp.concatenate([vp, vi], axis=0)[None]
        return diff_attend(qi[None], kk, vv, mask, lam)[0]

    return lax.map(one, (q, k, v, page_table))


def gla_scan(q, k, v, logg, s0):
    B, L = q.shape[:2]
    C = math.gcd(L, GLA_CHUNK)
    nc = L // C

    def chunks(t):
        return t.astype(jnp.float32).reshape(B, nc, C, GLA_HEADS, t.shape[-1]).transpose(1, 0, 3, 2, 4)

    tril = jnp.tril(jnp.ones((C, C), bool))

    def step(S, inp):
        qc, kc, vc, gc = inp
        b = jnp.cumsum(gc, axis=2)
        o = jnp.einsum('bhtk,bhkv->bhtv', qc * jnp.exp(b), S)
        gap = jnp.where(tril[:, :, None], b[:, :, :, None, :] - b[:, :, None, :, :], -jnp.inf)
        a = jnp.einsum('bhtk,bhsk,bhtsk->bhts', qc, kc, jnp.exp(gap))
        o = o + jnp.einsum('bhts,bhsv->bhtv', a, vc)
        b_end = b[:, :, -1:, :]
        S = jnp.exp(b_end[:, :, 0, :, None]) * S + jnp.einsum('bhsk,bhsv->bhkv', kc * jnp.exp(b_end - b), vc)
        return S, o

    S, o = lax.scan(step, s0.astype(jnp.float32), (chunks(q), chunks(k), chunks(v), chunks(logg)))
    o = o.transpose(1, 0, 3, 2, 4).reshape(B, L, GLA_HEADS, GLA_DV)
    return o, S.astype(s0.dtype)


def setup_inputs(seed: int = 0) -> dict:
    key = jax.random.key(seed)
    ks = jax.random.split(key, 32)
    f32 = jnp.float32
    n_pages = PAST_LEN // PAGE_SIZE
    n_used = DEC_BATCH * n_pages
    n_pool = n_used + max(1, n_used // 4)
    n_in = sum(IN_SIZES)

    def nrm(k, shape, scale):
        return jax.random.normal(k, shape, f32) * scale

    def gain(k, shape):
        return 1.0 + 0.02 * jax.random.normal(k, shape, f32)

    page_table = jax.random.permutation(ks[5], n_pool)[:n_used].reshape(DEC_BATCH, n_pages).astype(jnp.int32)
    return {
        'x_prompt': nrm(ks[0], (BATCH, SEQ, D_MODEL), 1.0),
        'x_sample': nrm(ks[1], (DEC_BATCH, DEC_SEQ, D_MODEL), 1.0),
        'cache_k': nrm(ks[2], (DEPTH, n_pool, PAGE_SIZE, DA_HEADS, 2 * DA_HD), 1.0),
        'cache_v': nrm(ks[3], (DEPTH, n_pool, PAGE_SIZE, DA_HEADS, DA_VD), 1.0),
        'state_gla': nrm(ks[4], (DEPTH, DEC_BATCH, GLA_HEADS, GLA_DK, GLA_DV), 1.0),
        'page_table': page_table,
        'w_in': nrm(ks[6], (DEPTH, D_MODEL, n_in), D_MODEL ** -0.5),
        'attn_q_norm': gain(ks[7], (DEPTH, DA_HD)),
        'attn_k_norm': gain(ks[8], (DEPTH, DA_HD)),
        'lam_q1': nrm(ks[9], (DEPTH, DA_HD), 0.1),
        'lam_k1': nrm(ks[10], (DEPTH, DA_HD), 0.1),
        'lam_q2': nrm(ks[11], (DEPTH, DA_HD), 0.1),
        'lam_k2': nrm(ks[12], (DEPTH, DA_HD), 0.1),
        'attn_out_norm': gain(ks[13], (DEPTH, DA_VD)),
        'gla_w_gate_up': nrm(ks[14], (DEPTH, GATE_RANK, GLA_HEADS * GLA_DK), GATE_RANK ** -0.5),
        'gla_b_gate': nrm(ks[15], (DEPTH, GLA_HEADS * GLA_DK), 0.1),
        'gla_out_norm': gain(ks[16], (DEPTH, GLA_DV)),
        'w_out': nrm(ks[17], (DEPTH, D_MODEL, D_MODEL), D_MODEL ** -0.5),
        'ffn1_norm': gain(ks[18], (DEPTH, D_MODEL)),
        'ffn1_w_gate': nrm(ks[19], (DEPTH, D_MODEL, D_FF), D_MODEL ** -0.5),
        'ffn1_w_up': nrm(ks[20], (DEPTH, D_MODEL, D_FF), D_MODEL ** -0.5),
        'ffn1_w_down': nrm(ks[21], (DEPTH, D_FF, D_MODEL), D_FF ** -0.5),
        'mix_norm': gain(ks[22], (DEPTH, D_MODEL)),
        'ffn2_norm': gain(ks[23], (DEPTH, D_MODEL)),
        'ffn2_w_gate': nrm(ks[24], (DEPTH, D_MODEL, D_FF), D_MODEL ** -0.5),
        'ffn2_w_up': nrm(ks[25], (DEPTH, D_MODEL, D_FF), D_MODEL ** -0.5),
        'ffn2_w_down': nrm(ks[26], (DEPTH, D_FF, D_MODEL), D_FF ** -0.5),
        'final_norm': gain(ks[27], (DEPTH, D_MODEL)),
    }


def reference(x_prompt, x_sample, cache_k, cache_v, state_gla, page_table, w_in, attn_q_norm, attn_k_norm,
              lam_q1, lam_k1, lam_q2, lam_k2, attn_out_norm, gla_w_gate_up, gla_b_gate, gla_out_norm, w_out,
              ffn1_norm, ffn1_w_gate, ffn1_w_up, ffn1_w_down, mix_norm, ffn2_norm, ffn2_w_gate, ffn2_w_up,
              ffn2_w_down, final_norm):
    f32 = jnp.float32
    split_points = [int(s) for s in np.cumsum(IN_SIZES)[:-1]]
    xp, xs = x_prompt, x_sample
    pos_p = jnp.arange(SEQ)
    pos_s = PAST_LEN + jnp.arange(DEC_SEQ)
    kp_l, vp_l, sp_l, ks_l, vs_l, ss_l = [], [], [], [], [], []
    for l in range(DEPTH):
        li = lambda_init(l)
        lam = (jnp.exp(jnp.sum(lam_q1[l].astype(f32) * lam_k1[l].astype(f32)))
               - jnp.exp(jnp.sum(lam_q2[l].astype(f32) * lam_k2[l].astype(f32))) + li)

        def run(x, pos, attend, s0):
            x = x + 0.5 * swiglu(rmsnorm(x, ffn1_norm[l]), ffn1_w_gate[l], ffn1_w_up[l], ffn1_w_down[l])
            h = rmsnorm(x, mix_norm[l])
            B, L = h.shape[:2]
            z = h @ w_in[l]
            qa, ka, va, qg, kg, vg, r, gd = jnp.split(z, split_points, axis=-1)
            qa = partial_rope(rmsnorm(qa.reshape(B, L, DA_HEADS, 2, DA_HD), attn_q_norm[l]), pos)
            ka = partial_rope(rmsnorm(ka.reshape(B, L, DA_HEADS, 2, DA_HD), attn_k_norm[l]), pos)
            va = va.reshape(B, L, DA_HEADS, DA_VD)
            o_a = attend(qa, ka, va, lam)
            o_a = (rmsnorm(o_a, attn_out_norm[l]) * (1.0 - li)).reshape(B, L, DA_WIDTH)
            qg = qg.reshape(B, L, GLA_HEADS, GLA_DK) * (GLA_DK ** -0.5)
            kg = kg.reshape(B, L, GLA_HEADS, GLA_DK)
            vg = vg.reshape(B, L, GLA_HEADS, GLA_DV)
            logg = (jax.nn.log_sigmoid((gd @ gla_w_gate_up[l] + gla_b_gate[l]).astype(f32)) / GATE_TAU)
            logg = logg.reshape(B, L, GLA_HEADS, GLA_DK)
            o_g, s_new = gla_scan(qg, kg, vg, logg, s0)
            o_g = rmsnorm(o_g.astype(h.dtype), gla_out_norm[l]) * jax.nn.silu(r.reshape(B, L, GLA_HEADS, GLA_DV))
            o_g = o_g.reshape(B, L, GLA_WIDTH)
            x = x + jnp.concatenate([o_a, o_g], axis=-1) @ w_out[l]
            x = x + 0.5 * swiglu(rmsnorm(x, ffn2_norm[l]), ffn2_w_gate[l], ffn2_w_up[l], ffn2_w_down[l])
            x = rmsnorm(x, final_norm[l])
            return x, ka.reshape(B, L, DA_HEADS, 2 * DA_HD), va, s_new

        def attend_sample(q, k, v, lam_):
            return sample_attention(q, k, v, lam_, cache_k, cache_v, page_table, l)

        xp, kp, vp, sp = run(xp, pos_p, prompt_attention,
                             jnp.zeros((BATCH, GLA_HEADS, GLA_DK, GLA_DV), xp.dtype))
        xs, kss, vss, sss = run(xs, pos_s, attend_sample, state_gla[l])
        kp_l.append(kp)
        vp_l.append(vp)
        sp_l.append(sp)
        ks_l.append(kss)
        vs_l.append(vss)
        ss_l.append(sss)
    return (xp, xs, jnp.stack(kp_l), jnp.stack(vp_l), jnp.stack(sp_l),
            jnp.stack(ks_l), jnp.stack(vs_l), jnp.stack(ss_l))
```

```python
import functools
import math

import numpy as np
import jax
import jax.numpy as jnp
from jax import lax
from jax.experimental import pallas as pl
from jax.experimental.pallas import tpu as pltpu

F32 = jnp.float32
BF16 = jnp.bfloat16
EPS = 1e-6
ROPE_THETA = 500000.0
GATE_TAU = 16.0
LANES = 128
VMEM_LIMIT = 60 * 1024 * 1024
NT_DIMS = (((1,), (1,)), ((), ()))
TN_DIMS = (((0,), (0,)), ((), ()))


def _params(*sem):
    return pltpu.CompilerParams(dimension_semantics=sem, vmem_limit_bytes=VMEM_LIMIT)


def _rms(x, gain):
    ms = jnp.mean(x * x, axis=-1, keepdims=True)
    return x * lax.rsqrt(ms + EPS) * gain


def _tile(n, want):
    t = min(n, want)
    while n % t:
        t //= 2
    return t


def _ffn_kernel(x_ref, g_ref, wgu_ref, wd_ref, pg_ref, *rest, tf, final):
    if final:
        y_ref, xn_ref = rest
    else:
        y_ref, h_ref, xn_ref = rest
    j = pl.program_id(1)

    @pl.when(j == 0)
    def _():
        xn_ref[...] = _rms(x_ref[...], g_ref[...]).astype(BF16)
        y_ref[...] = jnp.zeros_like(y_ref)

    gu = jnp.dot(xn_ref[...], wgu_ref[0], preferred_element_type=F32)
    g = gu[:, :tf]
    u = gu[:, tf:]
    h = (g * jax.nn.sigmoid(g) * u).astype(BF16)
    y_ref[...] += jnp.dot(h, wd_ref[...], preferred_element_type=F32)

    @pl.when(j == pl.num_programs(1) - 1)
    def _():
        y = x_ref[...] + 0.5 * y_ref[...]
        if final:
            y_ref[...] = _rms(y, pg_ref[...])
        else:
            y_ref[...] = y
            h_ref[...] = _rms(y, pg_ref[...]).astype(BF16)


def _ffn(x, norm_g, wgu, wd, post_g, *, final, tm=512):
    n, d = x.shape
    nf, _, tf2 = wgu.shape
    tf = tf2 // 2
    tm = _tile(n, tm)
    out_shape = [jax.ShapeDtypeStruct((n, d), F32)]
    out_specs = [pl.BlockSpec((tm, d), lambda i, j: (i, 0))]
    if not final:
        out_shape.append(jax.ShapeDtypeStruct((n, d), BF16))
        out_specs.append(pl.BlockSpec((tm, d), lambda i, j: (i, 0)))
    res = pl.pallas_call(
        functools.partial(_ffn_kernel, tf=tf, final=final),
        grid=(n // tm, nf),
        in_specs=[
            pl.BlockSpec((tm, d), lambda i, j: (i, 0), pipeline_mode=pl.Buffered(1)),
            pl.BlockSpec((1, d), lambda i, j: (0, 0)),
            pl.BlockSpec((1, d, tf2), lambda i, j: (j, 0, 0)),
            pl.BlockSpec((tf, d), lambda i, j: (j, 0)),
            pl.BlockSpec((1, d), lambda i, j: (0, 0)),
        ],
        out_specs=out_specs,
        out_shape=out_shape,
        scratch_shapes=[pltpu.VMEM((tm, d), BF16)],
        compiler_params=_params("parallel", "arbitrary"),
    )(x, norm_g, wgu, wd, post_g)
    return res[0] if final else (res[0], res[1])


def _proj_kernel(h_ref, w_ref, *outs):
    acc = jnp.dot(h_ref[...], w_ref[...], preferred_element_type=F32)
    for o in outs:
        o[...] = acc.astype(o.dtype)


def _proj(h, w, col0, ncols, dtypes, *, tm=512, tn=512):
    n, d = h.shape
    tm = _tile(n, tm)
    tn = _tile(ncols, tn)
    if ncols == w.shape[1]:
        off = 0
    else:
        assert col0 % tn == 0
        off = col0 // tn
    return pl.pallas_call(
        _proj_kernel,
        grid=(n // tm, ncols // tn),
        in_specs=[
            pl.BlockSpec((tm, d), lambda i, j: (i, 0)),
            pl.BlockSpec((d, tn), lambda i, j: (0, off + j)),
        ],
        out_specs=[pl.BlockSpec((tm, tn), lambda i, j: (i, j)) for _ in dtypes],
        out_shape=[jax.ShapeDtypeStruct((n, ncols), dt) for dt in dtypes],
        compiler_params=_params("parallel", "arbitrary"),
    )(h, w)


def _qk_kernel(h_ref, w_ref, gain_ref, c_ref, sa_ref, sb_ref, *outs, tn, rot):
    acc = jnp.dot(h_ref[...], w_ref[...], preferred_element_type=F32)
    gain = gain_ref[...]
    cos = c_ref[...]
    sa = sa_ref[...]
    sb = sb_ref[...]
    half = rot // 2
    for c in range(tn // LANES):
        xn = _rms(acc[:, c * LANES:(c + 1) * LANES], gain)
        y = xn * cos + pltpu.roll(xn, LANES - half, 1) * sa + pltpu.roll(xn, half, 1) * sb
        for o in outs:
            o[:, c * LANES:(c + 1) * LANES] = y.astype(o.dtype)


def _qk_proj(h, w, col0, ncols, gain, tables, dtypes, *, rot, tm=512, tn=512):
    n, d = h.shape
    tm = _tile(n, tm)
    tn = _tile(ncols, tn)
    cos, sa, sb = tables
    tm = _tile(cos.shape[0], tm)
    nt = cos.shape[0] // tm
    off = col0 // tn
    tspec = pl.BlockSpec((tm, LANES), lambda i, j: (i % nt, 0))
    return pl.pallas_call(
        functools.partial(_qk_kernel, tn=tn, rot=rot),
        grid=(n // tm, ncols // tn),
        in_specs=[
            pl.BlockSpec((tm, d), lambda i, j: (i, 0)),
            pl.BlockSpec((d, tn), lambda i, j: (0, off + j)),
            pl.BlockSpec((1, LANES), lambda i, j: (0, 0)),
            tspec, tspec, tspec,
        ],
        out_specs=[pl.BlockSpec((tm, tn), lambda i, j: (i, j)) for _ in dtypes],
        out_shape=[jax.ShapeDtypeStruct((n, ncols), dt) for dt in dtypes],
        compiler_params=_params("parallel", "arbitrary"),
    )(h, w, gain, cos, sa, sb)


def _rope_tables(pos, hd, rot):
    half = rot // 2
    inv = ROPE_THETA ** (-jnp.arange(half, dtype=F32) * 2.0 / rot)
    ang = pos.astype(F32)[:, None] * inv[None, :]
    cos, sin = jnp.cos(ang), jnp.sin(ang)
    n = pos.shape[0]
    ones = jnp.ones((n, hd - rot), F32)
    zeros_h = jnp.zeros((n, half), F32)
    zeros_r = jnp.zeros((n, hd - rot), F32)
    c = jnp.concatenate([cos, cos, ones], axis=1)
    sa = jnp.concatenate([-sin, zeros_h, zeros_r], axis=1)
    sb = jnp.concatenate([zeros_h, sin, zeros_r], axis=1)
    return c, sa, sb


def _lambda(lam_ref, li):
    lv = lam_ref[...]
    a = jnp.sum(lv[0:1] * lv[1:2], axis=-1, keepdims=True)
    b = jnp.sum(lv[2:3] * lv[3:4], axis=-1, keepdims=True)
    return jnp.exp(a) - jnp.exp(b) + li


def _pattn_kernel(q_ref, k_ref, v_ref, lam_ref, gn_ref, o_ref, *, t, hd, li):
    i = pl.program_id(2)
    q = q_ref[...]
    scale = hd ** -0.5
    lam = _lambda(lam_ref, li)
    vd = v_ref.shape[1]

    def block(kb, carry, masked):
        start = pl.multiple_of(kb * t, t)
        ks = k_ref[pl.ds(start, t), :]
        vs = v_ref[pl.ds(start, t), :]
        out = []
        for c in range(2):
            m, l, a = carry[3 * c:3 * c + 3]
            s = lax.dot_general(q[:, c * hd:(c + 1) * hd], ks[:, c * hd:(c + 1) * hd], NT_DIMS,
                                preferred_element_type=F32) * scale
            if masked:
                row = lax.broadcasted_iota(jnp.int32, (t, t), 0)
                col = lax.broadcasted_iota(jnp.int32, (t, t), 1)
                s = jnp.where(row >= col, s, -jnp.inf)
            m_new = jnp.maximum(m, jnp.max(s, axis=-1, keepdims=True))
            alpha = jnp.exp(m - m_new)
            p = jnp.exp(s - m_new)
            l = alpha * l + jnp.sum(p, axis=-1, keepdims=True)
            a = alpha * a + jnp.dot(p.astype(BF16), vs, preferred_element_type=F32)
            out += [m_new, l, a]
        return tuple(out)

    init = (jnp.full((t, 1), -jnp.inf, F32), jnp.zeros((t, 1), F32), jnp.zeros((t, vd), F32)) * 2
    carry = lax.fori_loop(0, i, lambda kb, c: block(kb, c, False), init)
    m0, l0, a0, m1, l1, a1 = block(i, carry, True)
    o = a0 / l0 - lam * (a1 / l1)
    o_ref[...] = (_rms(o, gn_ref[...]) * (1.0 - li)).astype(o_ref.dtype)


def _prompt_attention(q, k, v, lam_vecs, gn, *, nb, seq, hd, li, t=256):
    n, width = q.shape
    heads = width // (2 * hd)
    vd = v.shape[1] // heads
    t = _tile(seq, t)
    nq = seq // t
    return pl.pallas_call(
        functools.partial(_pattn_kernel, t=t, hd=hd, li=li),
        grid=(nb, heads, nq),
        in_specs=[
            pl.BlockSpec((t, 2 * hd), lambda b, h, i: (b * nq + i, h)),
            pl.BlockSpec((seq, 2 * hd), lambda b, h, i: (b, h)),
            pl.BlockSpec((seq, vd), lambda b, h, i: (b, h)),
            pl.BlockSpec((4, hd), lambda b, h, i: (0, 0)),
            pl.BlockSpec((1, vd), lambda b, h, i: (0, 0)),
        ],
        out_specs=pl.BlockSpec((t, vd), lambda b, h, i: (b * nq + i, h)),
        out_shape=jax.ShapeDtypeStruct((n, heads * vd), BF16),
        compiler_params=_params("parallel", "parallel", "arbitrary"),
    )(q, k, v, lam_vecs, gn)


def _sattn_kernel(pt_ref, q_ref, kn_ref, vn_ref, lam_ref, gn_ref, *rest, G, heads, hd, ld, li):
    kp = rest[:G]
    vp = rest[G:2 * G]
    o_ref, m_ref, l_ref, acc_ref = rest[2 * G:]
    p = pl.program_id(1)
    scale = hd ** -0.5
    w2 = 2 * hd
    vd = acc_ref.shape[2]
    rows = 2 * ld
    row = lax.broadcasted_iota(jnp.int32, (rows, w2), 0)
    lane = lax.broadcasted_iota(jnp.int32, (rows, w2), 1)
    own_map = (row < ld) == (lane < hd)
    qi = lax.broadcasted_iota(jnp.int32, (rows, 1), 0) % ld

    def qbd(h):
        qh = q_ref[0][:, h * w2:(h + 1) * w2].astype(F32)
        return jnp.where(own_map, jnp.concatenate([qh, qh], axis=0), 0.0)

    @pl.when(p == 0)
    def _():
        for h in range(heads):
            qf = qbd(h)
            kn = kn_ref[0][:, h * w2:(h + 1) * w2]
            vn = vn_ref[0][:, h * vd:(h + 1) * vd]
            s = [jnp.where(qi >= j, jnp.sum(qf * kn[j:j + 1, :], axis=-1, keepdims=True) * scale, -jnp.inf)
                 for j in range(ld)]
            m = functools.reduce(jnp.maximum, s)
            pj = [jnp.exp(sj - m) for sj in s]
            m_ref[h] = m
            l_ref[h] = functools.reduce(jnp.add, pj)
            acc_ref[h] = functools.reduce(jnp.add, [pj[j] * vn[j:j + 1, :] for j in range(ld)])

    for h in range(heads):
        qb = qbd(h).astype(BF16)
        s = jnp.concatenate(
            [lax.dot_general(qb, kp[g][0][:, h * w2:(h + 1) * w2].astype(BF16), NT_DIMS,
                             preferred_element_type=F32) for g in range(G)], axis=1) * scale
        m_old = m_ref[h]
        m_new = jnp.maximum(m_old, jnp.max(s, axis=-1, keepdims=True))
        alpha = jnp.exp(m_old - m_new)
        pr = jnp.exp(s - m_new)
        ps = pr.shape[1] // G
        pv = functools.reduce(jnp.add, [
            jnp.dot(pr[:, g * ps:(g + 1) * ps].astype(BF16), vp[g][0][:, h * vd:(h + 1) * vd].astype(BF16),
                    preferred_element_type=F32) for g in range(G)])
        m_ref[h] = m_new
        l_ref[h] = alpha * l_ref[h] + jnp.sum(pr, axis=-1, keepdims=True)
        acc_ref[h] = alpha * acc_ref[h] + pv

    @pl.when(p == pl.num_programs(1) - 1)
    def _():
        lam = _lambda(lam_ref, li)
        for h in range(heads):
            w = acc_ref[h] / l_ref[h]
            o = w[:ld] - lam * w[ld:]
            o_ref[0, :, h * vd:(h + 1) * vd] = _rms(o, gn_ref[...]) * (1.0 - li)


def _sample_attention(q, kn, vn, cache_k, cache_v, page_table, lam_vecs, gn, *, hd, li, G=4):
    nb, ld, width = q.shape
    heads = width // (2 * hd)
    vwidth = vn.shape[2]
    vd = vwidth // heads
    page = cache_k.shape[1]
    n_pages = page_table.shape[1]
    G = _tile(n_pages, G)

    def page_spec(w, g):
        return pl.BlockSpec((1, page, w), lambda b, p, pt: (pt[b, p * G + g], 0, 0))

    grid_spec = pltpu.PrefetchScalarGridSpec(
        num_scalar_prefetch=1,
        grid=(nb, n_pages // G),
        in_specs=[
            pl.BlockSpec((1, ld, width), lambda b, p, pt: (b, 0, 0)),
            pl.BlockSpec((1, ld, width), lambda b, p, pt: (b, 0, 0)),
            pl.BlockSpec((1, ld, vwidth), lambda b, p, pt: (b, 0, 0)),
            pl.BlockSpec((4, hd), lambda b, p, pt: (0, 0)),
            pl.BlockSpec((1, vd), lambda b, p, pt: (0, 0)),
        ] + [page_spec(width, g) for g in range(G)] + [page_spec(vwidth, g) for g in range(G)],
        out_specs=pl.BlockSpec((1, ld, vwidth), lambda b, p, pt: (b, 0, 0)),
        scratch_shapes=[
            pltpu.VMEM((heads, 2 * ld, 1), F32),
            pltpu.VMEM((heads, 2 * ld, 1), F32),
            pltpu.VMEM((heads, 2 * ld, vd), F32),
        ],
    )
    return pl.pallas_call(
        functools.partial(_sattn_kernel, G=G, heads=heads, hd=hd, ld=ld, li=li),
        grid_spec=grid_spec,
        out_shape=jax.ShapeDtypeStruct((nb, ld, vwidth), F32),
        compiler_params=_params("parallel", "arbitrary"),
    )(page_table, q, kn, vn, lam_vecs, gn, *([cache_k] * G), *([cache_v] * G))


def _gla_tables(c):
    t = np.arange(c)[:, None]
    u = np.arange(c)[None, :]
    mats = [u <= t, u > t]
    masks = []
    m = c // 2
    while m >= 1:
        mats.append((u > (t // m) * m) & (u <= t))
        mats.append((u > t) & (u <= (t // m + 1) * m))
        masks.append(((t // m) % 2 == 1) & ((u // m) == (t // m) - 1))
        m //= 2
    masks.append(t == u)
    return (np.concatenate(mats, 0).astype(np.float32), np.stack(masks, 0).astype(np.float32))


def _gla_kernel(q_ref, k_ref, v_ref, gd_ref, r_ref, s0_ref, wg_ref, bg_ref, gn_ref, mat_ref, mask_ref,
                o_ref, sout_ref, s_ref, *, c, valid, qscale):
    n = pl.program_id(2)

    @pl.when(n == 0)
    def _():
        s_ref[...] = s0_ref[0, 0]

    x = jnp.dot(gd_ref[...].astype(BF16), wg_ref[...], preferred_element_type=F32) + bg_ref[...]
    logg = (jnp.minimum(x, 0.0) - jnp.log1p(jnp.exp(-jnp.abs(x)))) * (1.0 / GATE_TAU)
    if valid < c:
        logg = jnp.where(lax.broadcasted_iota(jnp.int32, logg.shape, 0) < valid, logg, 0.0)
    hi = logg.astype(BF16)
    r1 = logg - hi.astype(F32)
    mid = r1.astype(BF16)
    lo = (r1 - mid.astype(F32)).astype(BF16)
    mats = mat_ref[...]
    e = jnp.exp(jnp.dot(mats, hi, preferred_element_type=F32)
                + jnp.dot(mats, mid, preferred_element_type=F32)
                + jnp.dot(mats, lo, preferred_element_type=F32))

    q = q_ref[...] * qscale
    k = k_ref[...]
    v = v_ref[...].astype(BF16)
    s_old = s_ref[...]
    o = jnp.dot((q * e[0:c]).astype(BF16), s_old.astype(BF16), preferred_element_type=F32)

    nlev = mask_ref.shape[0] - 1
    a = jnp.zeros((c, c), F32)
    for lv in range(nlev):
        qt = (q * e[(2 + 2 * lv) * c:(3 + 2 * lv) * c]).astype(BF16)
        kt = (k * e[(3 + 2 * lv) * c:(4 + 2 * lv) * c]).astype(BF16)
        a = a + mask_ref[lv] * lax.dot_general(qt, kt, NT_DIMS, preferred_element_type=F32)
    a = a + mask_ref[nlev] * lax.dot_general(q.astype(BF16), k.astype(BF16), NT_DIMS, preferred_element_type=F32)
    o = o + jnp.dot(a.astype(BF16), v, preferred_element_type=F32)

    dk = q.shape[1]
    dv = v.shape[1]
    decay_col = jnp.broadcast_to(e[c - 1:c], (LANES, dk)).T
    decay = jnp.concatenate([decay_col] * (dv // LANES), axis=1)
    kend = (k * e[c:2 * c]).astype(BF16)
    s_new = decay * s_old + lax.dot_general(kend, v, TN_DIMS, preferred_element_type=F32)
    s_ref[...] = s_new

    r = r_ref[...]
    o_ref[...] = (_rms(o, gn_ref[...]) * (r * jax.nn.sigmoid(r))).astype(o_ref.dtype)

    @pl.when(n == pl.num_programs(2) - 1)
    def _():
        sout_ref[0, 0] = s_new


def _gla(z, zcols, gd, s0, wg, bg, gn, *, nb, seq, c, valid, out_dtype):
    _, heads, dk, dv = s0.shape
    rank = gd.shape[1]
    nc = seq // c
    qc, kc, vc, rc = zcols
    mats, masks = _gla_tables(c)
    mats = jnp.asarray(mats, BF16)
    masks = jnp.asarray(masks, F32)

    def zspec(w, col0):
        off = col0 // w
        return pl.BlockSpec((c, w), lambda b, h, n: (b * nc + n, off + h))

    return pl.pallas_call(
        functools.partial(_gla_kernel, c=c, valid=valid, qscale=dk ** -0.5),
        grid=(nb, heads, nc),
        in_specs=[
            zspec(dk, qc), zspec(dk, kc), zspec(dv, vc),
            pl.BlockSpec((c, rank), lambda b, h, n: (b * nc + n, 0)),
            zspec(dv, rc),
            pl.BlockSpec((1, 1, dk, dv), lambda b, h, n: (b, h, 0, 0)),
            pl.BlockSpec((rank, dk), lambda b, h, n: (0, h)),
            pl.BlockSpec((1, dk), lambda b, h, n: (0, h)),
            pl.BlockSpec((1, dv), lambda b, h, n: (0, 0)),
            pl.BlockSpec(mats.shape, lambda b, h, n: (0, 0)),
            pl.BlockSpec(masks.shape, lambda b, h, n: (0, 0, 0)),
        ],
        out_specs=[
            pl.BlockSpec((c, dv), lambda b, h, n: (b * nc + n, h)),
            pl.BlockSpec((1, 1, dk, dv), lambda b, h, n: (b, h, 0, 0)),
        ],
        out_shape=[
            jax.ShapeDtypeStruct((nb * seq, heads * dv), out_dtype),
            jax.ShapeDtypeStruct(s0.shape, F32),
        ],
        scratch_shapes=[pltpu.VMEM((dk, dv), F32)],
        compiler_params=_params("parallel", "parallel", "arbitrary"),
    )(z, z, z, gd, z, s0, wg, bg, gn, mats, masks)


def _oproj_kernel(oa_ref, og_ref, wa_ref, wg_ref, x_ref, y_ref):
    acc = jnp.dot(oa_ref[...], wa_ref[...], preferred_element_type=F32)
    acc = acc + jnp.dot(og_ref[...], wg_ref[...], preferred_element_type=F32)
    y_ref[...] = x_ref[...] + acc


def _oproj(oa, og, w, x, *, tm=512, tn=512):
    n, d = x.shape
    wa_rows = oa.shape[1]
    wg_rows = og.shape[1]
    assert wa_rows == wg_rows
    tm = _tile(n, tm)
    tn = _tile(d, tn)
    return pl.pallas_call(
        _oproj_kernel,
        grid=(n // tm, d // tn),
        in_specs=[
            pl.BlockSpec((tm, wa_rows), lambda i, j: (i, 0)),
            pl.BlockSpec((tm, wg_rows), lambda i, j: (i, 0)),
            pl.BlockSpec((wa_rows, tn), lambda i, j: (0, j)),
            pl.BlockSpec((wg_rows, tn), lambda i, j: (1, j)),
            pl.BlockSpec((tm, tn), lambda i, j: (i, j)),
        ],
        out_specs=pl.BlockSpec((tm, tn), lambda i, j: (i, j)),
        out_shape=jax.ShapeDtypeStruct((n, d), F32),
        compiler_params=_params("parallel", "arbitrary"),
    )(oa, og, w, w, x)


def _ffn_weights(w_gate, w_up, w_down, tf=256):
    d, dff = w_gate.shape
    tf = _tile(dff, tf)
    nf = dff // tf
    g = w_gate.astype(BF16).reshape(d, nf, tf)
    u = w_up.astype(BF16).reshape(d, nf, tf)
    wgu = jnp.concatenate([g, u], axis=2).transpose(1, 0, 2)
    return wgu, w_down.astype(BF16)


def _row(v):
    return v.reshape(1, -1).astype(F32)


def kernel(x_prompt, x_sample, cache_k, cache_v, state_gla, page_table, w_in, attn_q_norm, attn_k_norm, lam_q1, lam_k1, lam_q2, lam_k2, attn_out_norm, gla_w_gate_up, gla_b_gate, gla_out_norm, w_out, ffn1_norm, ffn1_w_gate, ffn1_w_up, ffn1_w_down, mix_norm, ffn2_norm, ffn2_w_gate, ffn2_w_up, ffn2_w_down, final_norm):
    nbp, seq, d = x_prompt.shape
    nbs, ld, _ = x_sample.shape
    depth, n_pool, page, heads_a, w2 = cache_k.shape
    hd = w2 // 2
    vd = cache_v.shape[-1]
    _, _, heads_g, dk, dv = state_gla.shape
    rank = gla_w_gate_up.shape[1]
    past = page_table.shape[1] * page
    rot = hd // 4
    wq = heads_a * w2
    wv = heads_a * vd
    wgq = heads_g * dk
    wgv = heads_g * dv
    c_q, c_k, c_v = 0, wq, 2 * wq
    c_rest = 2 * wq + wv
    w_rest = 2 * wgq + 2 * wgv
    c_gd = c_rest + w_rest
    zcols = (0, wgq, 2 * wgq, 2 * wgq + wgv)

    tab_p = _rope_tables(jnp.arange(seq), hd, rot)
    tab_s = _rope_tables(past + jnp.arange(nbs * ld) % ld, hd, rot)
    chunk = math.gcd(seq, 64)
    ld_pad = -(-ld // 8) * 8

    xp = x_prompt.reshape(nbp * seq, d)
    xs = x_sample.reshape(nbs * ld, d)
    outs = [[] for _ in range(6)]
    for l in range(depth):
        li = 0.8 - 0.6 * math.exp(-0.3 * l)
        wgu1, wd1 = _ffn_weights(ffn1_w_gate[l], ffn1_w_up[l], ffn1_w_down[l])
        wgu2, wd2 = _ffn_weights(ffn2_w_gate[l], ffn2_w_up[l], ffn2_w_down[l])
        w_in_b = w_in[l].astype(BF16)
        w_gd = w_in_b[:, c_gd:]
        w_out_b = w_out[l].astype(BF16)
        lam_vecs = jnp.stack([lam_q1[l], lam_k1[l], lam_q2[l], lam_k2[l]]).astype(F32)
        gq, gk, gn_a = _row(attn_q_norm[l]), _row(attn_k_norm[l]), _row(attn_out_norm[l])
        wg = gla_w_gate_up[l].astype(BF16)
        bg, gn_g = _row(gla_b_gate[l]), _row(gla_out_norm[l])

        def dense_in(x, tables, q_dtype):
            x1, h = _ffn(x, _row(ffn1_norm[l]), wgu1, wd1, _row(mix_norm[l]), final=False)
            (qa,) = _qk_proj(h, w_in_b, c_q, wq, gq, tables, [q_dtype], rot=rot)
            ka, ka_b = _qk_proj(h, w_in_b, c_k, wq, gk, tables, [F32, BF16], rot=rot)
            va, va_b = _proj(h, w_in_b, c_v, wv, [F32, BF16])
            (z,) = _proj(h, w_in_b, c_rest, w_rest, [F32])
            (gd,) = _proj(h, w_gd, 0, rank, [F32])
            return x1, qa, ka, ka_b, va, va_b, z, gd

        def dense_out(x1, oa, og):
            x2 = _oproj(oa, og, w_out_b, x1)
            return _ffn(x2, _row(ffn2_norm[l]), wgu2, wd2, _row(final_norm[l]), final=True)

        x1, qa, ka, ka_b, va, va_b, z, gd = dense_in(xp, tab_p, BF16)
        oa = _prompt_attention(qa, ka_b, va_b, lam_vecs, gn_a, nb=nbp, seq=seq, hd=hd, li=li)
        og, s_p = _gla(z, zcols, gd, jnp.zeros((nbp, heads_g, dk, dv), F32), wg, bg, gn_g,
                       nb=nbp, seq=seq, c=chunk, valid=chunk, out_dtype=BF16)
        xp = dense_out(x1, oa, og)
        outs[0].append(ka.reshape(nbp, seq, heads_a, w2))
        outs[1].append(va.reshape(nbp, seq, heads_a, vd))
        outs[2].append(s_p)

        x1, qa, ka, ka_b, va, va_b, z, gd = dense_in(xs, tab_s, F32)
        oa = _sample_attention(qa.reshape(nbs, ld, wq), ka.reshape(nbs, ld, wq), va.reshape(nbs, ld, wv),
                               cache_k[l].reshape(n_pool, page, wq), cache_v[l].reshape(n_pool, page, wv),
                               page_table, lam_vecs, gn_a, hd=hd, li=li)
        pad = lambda a: jnp.pad(a.reshape(nbs, ld, -1), ((0, 0), (0, ld_pad - ld), (0, 0))).reshape(nbs * ld_pad, -1)
        og, s_s = _gla(pad(z), zcols, pad(gd), state_gla[l], wg, bg, gn_g,
                       nb=nbs, seq=ld_pad, c=ld_pad, valid=ld, out_dtype=F32)
        og = og.reshape(nbs, ld_pad, wgv)[:, :ld].reshape(nbs * ld, wgv)
        xs = dense_out(x1, oa.reshape(nbs * ld, wv).astype(BF16), og.astype(BF16))
        outs[3].append(ka.reshape(nbs, ld, heads_a, w2))
        outs[4].append(va.reshape(nbs, ld, heads_a, vd))
        outs[5].append(s_s)

    return (xp.reshape(nbp, seq, d), xs.reshape(nbs, ld, d), jnp.stack(outs[0]), jnp.stack(outs[1]),
            jnp.stack(outs[2]), jnp.stack(outs[3]), jnp.stack(outs[4]), jnp.stack(outs[5]))
```

```python
import functools
import math

import numpy as np
import jax
import jax.numpy as jnp
from jax import lax
from jax.experimental import pallas as pl
from jax.experimental.pallas import tpu as pltpu

F32 = jnp.float32
BF16 = jnp.bfloat16
EPS = 1e-6
ROPE_THETA = 500000.0
GATE_TAU = 16.0
GLA_CHUNK = 128
LANES = 128
MXU_COLS = 256
VMEM_LIMIT = 60 * 1024 * 1024
NT_DIMS = (((1,), (1,)), ((), ()))
TN_DIMS = (((0,), (0,)), ((), ()))


def _params(*sem):
    return pltpu.CompilerParams(dimension_semantics=sem, vmem_limit_bytes=VMEM_LIMIT)


def _rms(x, gain):
    ms = jnp.mean(x * x, axis=-1, keepdims=True)
    return x * lax.rsqrt(ms + EPS) * gain


def _tile(n, want):
    t = min(n, want)
    while n % t:
        t //= 2
    return t


def _ffn_kernel(x_ref, g_ref, wg_ref, wu_ref, wd_ref, pg_ref, *rest, final):
    if final:
        y_ref, xn_ref = rest
    else:
        y_ref, h_ref, xn_ref = rest
    j = pl.program_id(1)

    @pl.when(j == 0)
    def _():
        xn_ref[...] = _rms(x_ref[...], g_ref[...]).astype(BF16)
        y_ref[...] = jnp.zeros_like(y_ref)

    xn = xn_ref[...]
    g = jnp.dot(xn, wg_ref[...], preferred_element_type=F32)
    u = jnp.dot(xn, wu_ref[...], preferred_element_type=F32)
    h = (g * jax.nn.sigmoid(g) * u).astype(BF16)
    y_ref[...] += jnp.dot(h, wd_ref[...], preferred_element_type=F32)

    @pl.when(j == pl.num_programs(1) - 1)
    def _():
        y = x_ref[...] + 0.5 * y_ref[...]
        if final:
            y_ref[...] = _rms(y, pg_ref[...])
        else:
            y_ref[...] = y
            h_ref[...] = _rms(y, pg_ref[...]).astype(BF16)


def _ffn(x, norm_g, wg, wu, wd, post_g, *, final, tm=512, tf=256):
    n, d = x.shape
    dff = wd.shape[0]
    tf = _tile(dff, tf)
    nf = dff // tf
    tm = _tile(n, tm)
    out_shape = [jax.ShapeDtypeStruct((n, d), F32)]
    out_specs = [pl.BlockSpec((tm, d), lambda i, j: (i, 0))]
    if not final:
        out_shape.append(jax.ShapeDtypeStruct((n, d), BF16))
        out_specs.append(pl.BlockSpec((tm, d), lambda i, j: (i, 0)))
    res = pl.pallas_call(
        functools.partial(_ffn_kernel, final=final),
        grid=(n // tm, nf),
        in_specs=[
            pl.BlockSpec((tm, d), lambda i, j: (i, 0), pipeline_mode=pl.Buffered(1)),
            pl.BlockSpec((1, d), lambda i, j: (0, 0)),
            pl.BlockSpec((d, tf), lambda i, j: (0, j)),
            pl.BlockSpec((d, tf), lambda i, j: (0, j)),
            pl.BlockSpec((tf, d), lambda i, j: (j, 0)),
            pl.BlockSpec((1, d), lambda i, j: (0, 0)),
        ],
        out_specs=out_specs,
        out_shape=out_shape,
        scratch_shapes=[pltpu.VMEM((tm, d), BF16)],
        compiler_params=_params("parallel", "arbitrary"),
    )(x, norm_g, wg, wu, wd, post_g)
    return res[0] if final else (res[0], res[1])


def _proj_kernel(h_ref, w_ref, *outs):
    acc = jnp.dot(h_ref[...], w_ref[...], preferred_element_type=F32)
    for o in outs:
        o[...] = acc.astype(o.dtype)


def _proj(h, w, col0, ncols, dtypes, *, tm=1024, tn=512):
    n, d = h.shape
    tm = _tile(n, tm)
    tn = _tile(ncols, tn)
    if ncols == w.shape[1]:
        off = 0
    else:
        assert col0 % tn == 0
        off = col0 // tn
    return pl.pallas_call(
        _proj_kernel,
        grid=(n // tm, ncols // tn),
        in_specs=[
            pl.BlockSpec((tm, d), lambda i, j: (i, 0)),
            pl.BlockSpec((d, tn), lambda i, j: (0, off + j)),
        ],
        out_specs=[pl.BlockSpec((tm, tn), lambda i, j: (i, j)) for _ in dtypes],
        out_shape=[jax.ShapeDtypeStruct((n, ncols), dt) for dt in dtypes],
        compiler_params=_params("parallel", "arbitrary"),
    )(h, w)


def _qk_kernel(h_ref, w_ref, gain_ref, c_ref, sa_ref, sb_ref, *outs, tn, rot):
    h = h_ref[...]
    gain = gain_ref[...]
    cos = c_ref[...]
    sa = sa_ref[...]
    sb = sb_ref[...]
    half = rot // 2
    sub = min(tn, MXU_COLS)
    for s0 in range(0, tn, sub):
        acc = jnp.dot(h, w_ref[:, s0:s0 + sub], preferred_element_type=F32)
        for c0 in range(0, sub, LANES):
            xn = _rms(acc[:, c0:c0 + LANES], gain)
            y = xn * cos + pltpu.roll(xn, LANES - half, 1) * sa + pltpu.roll(xn, half, 1) * sb
            for o in outs:
                o[:, s0 + c0:s0 + c0 + LANES] = y.astype(o.dtype)


def _qk_proj(h, w, col0, ncols, gain, tables, dtypes, *, rot, tm=1024, tn=512):
    n, d = h.shape
    tm = _tile(n, tm)
    tn = _tile(ncols, tn)
    cos, sa, sb = tables
    tm = _tile(cos.shape[0], tm)
    nt = cos.shape[0] // tm
    off = col0 // tn
    tspec = pl.BlockSpec((tm, LANES), lambda i, j: (i % nt, 0))
    return pl.pallas_call(
        functools.partial(_qk_kernel, tn=tn, rot=rot),
        grid=(n // tm, ncols // tn),
        in_specs=[
            pl.BlockSpec((tm, d), lambda i, j: (i, 0)),
            pl.BlockSpec((d, tn), lambda i, j: (0, off + j)),
            pl.BlockSpec((1, LANES), lambda i, j: (0, 0)),
            tspec, tspec, tspec,
        ],
        out_specs=[pl.BlockSpec((tm, tn), lambda i, j: (i, j)) for _ in dtypes],
        out_shape=[jax.ShapeDtypeStruct((n, ncols), dt) for dt in dtypes],
        compiler_params=_params("parallel", "arbitrary"),
    )(h, w, gain, cos, sa, sb)


def _rope_tables(pos, hd, rot):
    half = rot // 2
    inv = ROPE_THETA ** (-jnp.arange(half, dtype=F32) * 2.0 / rot)
    ang = pos.astype(F32)[:, None] * inv[None, :]
    cos, sin = jnp.cos(ang), jnp.sin(ang)
    n = pos.shape[0]
    ones = jnp.ones((n, hd - rot), F32)
    zeros_h = jnp.zeros((n, half), F32)
    zeros_r = jnp.zeros((n, hd - rot), F32)
    c = jnp.concatenate([cos, cos, ones], axis=1)
    sa = jnp.concatenate([-sin, zeros_h, zeros_r], axis=1)
    sb = jnp.concatenate([zeros_h, sin, zeros_r], axis=1)
    return c, sa, sb


def _lambda(lam_ref, li):
    lv = lam_ref[...]
    a = jnp.sum(lv[0:1] * lv[1:2], axis=-1, keepdims=True)
    b = jnp.sum(lv[2:3] * lv[3:4], axis=-1, keepdims=True)
    return jnp.exp(a) - jnp.exp(b) + li


def _pattn_kernel(q_ref, k_ref, v_ref, lam_ref, gn_ref, o_ref, *, t, nq, hd, li):
    i = pl.program_id(2)
    scale = hd ** -0.5
    lam = _lambda(lam_ref, li)
    row = lax.broadcasted_iota(jnp.int32, (t, t), 0)
    col = lax.broadcasted_iota(jnp.int32, (t, t), 1)

    def tile(n):
        past = n * t
        q = q_ref[...]
        k = k_ref[0:past + t, :]
        p = []
        for c in range(2):
            s = lax.dot_general(q[:, c * hd:(c + 1) * hd], k[:, c * hd:(c + 1) * hd], NT_DIMS,
                                preferred_element_type=F32) * scale
            diag = jnp.where(row >= col, s[:, past:], -jnp.inf)
            s = jnp.concatenate([s[:, :past], diag], axis=1) if past else diag
            e = jnp.exp(s - jnp.max(s, axis=-1, keepdims=True))
            p.append(e * (1.0 / jnp.sum(e, axis=-1, keepdims=True)))
        w = (p[0] - lam * p[1]).astype(BF16)
        o = jnp.dot(w, v_ref[0:past + t, :], preferred_element_type=F32)
        o_ref[...] = (_rms(o, gn_ref[...]) * (1.0 - li)).astype(o_ref.dtype)

    for n in range(nq):
        pl.when(i == n)(functools.partial(tile, n))


def _prompt_attention(q, k, v, lam_vecs, gn, *, nb, seq, hd, li, t=256):
    n, width = q.shape
    heads = width // (2 * hd)
    vd = v.shape[1] // heads
    t = _tile(seq, t)
    nq = seq // t
    return pl.pallas_call(
        functools.partial(_pattn_kernel, t=t, nq=nq, hd=hd, li=li),
        grid=(nb, heads, nq),
        in_specs=[
            pl.BlockSpec((t, 2 * hd), lambda b, h, i: (b * nq + i, h)),
            pl.BlockSpec((seq, 2 * hd), lambda b, h, i: (b, h)),
            pl.BlockSpec((seq, vd), lambda b, h, i: (b, h)),
            pl.BlockSpec((4, hd), lambda b, h, i: (0, 0)),
            pl.BlockSpec((1, vd), lambda b, h, i: (0, 0)),
        ],
        out_specs=pl.BlockSpec((t, vd), lambda b, h, i: (b * nq + i, h)),
        out_shape=jax.ShapeDtypeStruct((n, heads * vd), BF16),
        compiler_params=_params("parallel", "parallel", "arbitrary"),
    )(q, k, v, lam_vecs, gn)


def _sattn_kernel(pt_ref, q_ref, kn_ref, vn_ref, lam_ref, gn_ref, *rest, G, heads, hd, ld, li):
    kp = rest[:G]
    vp = rest[G:2 * G]
    o_ref, qm_ref, m_ref, l_ref, acc_ref = rest[2 * G:]
    p = pl.program_id(1)
    scale = hd ** -0.5
    w2 = 2 * hd
    vd = acc_ref.shape[1]
    rows = 2 * ld
    page = kp[0].shape[1]
    pcols = page * heads

    @pl.when(p == 0)
    def _():
        row = lax.broadcasted_iota(jnp.int32, (rows, w2), 0)
        lane = lax.broadcasted_iota(jnp.int32, (rows, w2), 1)
        own_map = (row < ld) == (lane < hd)
        qi = lax.broadcasted_iota(jnp.int32, (rows, 1), 0) % ld
        qm, m0, l0, a0 = [], [], [], []
        for h in range(heads):
            qh = q_ref[0][:, h * w2:(h + 1) * w2]
            qf = jnp.where(own_map, jnp.concatenate([qh, qh], axis=0), 0.0)
            kn = kn_ref[0][:, h * w2:(h + 1) * w2]
            vn = vn_ref[0][:, h * vd:(h + 1) * vd]
            s = [jnp.where(qi >= j, jnp.sum(qf * kn[j:j + 1, :], axis=-1, keepdims=True) * scale, -jnp.inf)
                 for j in range(ld)]
            m = functools.reduce(jnp.maximum, s)
            pj = [jnp.exp(sj - m) for sj in s]
            qm.append(qf)
            m0.append(m)
            l0.append(functools.reduce(jnp.add, pj))
            a0.append(functools.reduce(jnp.add, [pj[j] * vn[j:j + 1, :] for j in range(ld)]))
        qm_ref[...] = jnp.concatenate(qm, axis=0).astype(BF16)
        m_ref[...] = jnp.concatenate(m0, axis=0)
        l_ref[...] = jnp.concatenate(l0, axis=0)
        acc_ref[...] = jnp.concatenate(a0, axis=0)

    nr = heads * rows
    qmat = qm_ref[...]
    same_head = (lax.broadcasted_iota(jnp.int32, (nr, pcols), 0) // rows
                 == lax.broadcasted_iota(jnp.int32, (nr, pcols), 1) % heads)
    s = jnp.concatenate(
        [jnp.where(same_head,
                   lax.dot_general(qmat, kp[g][0].reshape(pcols, w2).astype(BF16), NT_DIMS,
                                   preferred_element_type=F32) * scale, -jnp.inf)
         for g in range(G)], axis=1)
    m_old = m_ref[...]
    m_new = jnp.maximum(m_old, jnp.max(s, axis=-1, keepdims=True))
    alpha = jnp.exp(m_old - m_new)
    pr = jnp.exp(s - m_new)
    pv = functools.reduce(jnp.add, [
        jnp.dot(pr[:, g * pcols:(g + 1) * pcols].astype(BF16), vp[g][0].reshape(pcols, vd).astype(BF16),
                preferred_element_type=F32) for g in range(G)])
    m_ref[...] = m_new
    l_ref[...] = alpha * l_ref[...] + jnp.sum(pr, axis=-1, keepdims=True)
    acc_ref[...] = alpha * acc_ref[...] + pv

    @pl.when(p == pl.num_programs(1) - 1)
    def _():
        lam = _lambda(lam_ref, li)
        w = acc_ref[...] / l_ref[...]
        for h in range(heads):
            o = w[h * rows:h * rows + ld] - lam * w[h * rows + ld:(h + 1) * rows]
            o_ref[0, :, h * vd:(h + 1) * vd] = _rms(o, gn_ref[...]) * (1.0 - li)


def _sample_attention(q, kn, vn, cache_k, cache_v, page_table, lam_vecs, gn, *, pool0, hd, li, G=8):
    nb, ld, width = q.shape
    _, page, heads, w2 = cache_k.shape
    vwidth = vn.shape[2]
    vd = vwidth // heads
    n_pages = page_table.shape[1]
    G = _tile(n_pages, G)

    def page_spec(w, g):
        return pl.BlockSpec((1, page, heads, w), lambda b, p, pt: (pool0 + pt[b, p * G + g], 0, 0, 0))

    grid_spec = pltpu.PrefetchScalarGridSpec(
        num_scalar_prefetch=1,
        grid=(nb, n_pages // G),
        in_specs=[
            pl.BlockSpec((1, ld, width), lambda b, p, pt: (b, 0, 0)),
            pl.BlockSpec((1, ld, width), lambda b, p, pt: (b, 0, 0)),
            pl.BlockSpec((1, ld, vwidth), lambda b, p, pt: (b, 0, 0)),
            pl.BlockSpec((4, hd), lambda b, p, pt: (0, 0)),
            pl.BlockSpec((1, vd), lambda b, p, pt: (0, 0)),
        ] + [page_spec(w2, g) for g in range(G)] + [page_spec(vd, g) for g in range(G)],
        out_specs=pl.BlockSpec((1, ld, vwidth), lambda b, p, pt: (b, 0, 0)),
        scratch_shapes=[
            pltpu.VMEM((heads * 2 * ld, w2), BF16),
            pltpu.VMEM((heads * 2 * ld, 1), F32),
            pltpu.VMEM((heads * 2 * ld, 1), F32),
            pltpu.VMEM((heads * 2 * ld, vd), F32),
        ],
    )
    return pl.pallas_call(
        functools.partial(_sattn_kernel, G=G, heads=heads, hd=hd, ld=ld, li=li),
        grid_spec=grid_spec,
        out_shape=jax.ShapeDtypeStruct((nb, ld, vwidth), F32),
        compiler_params=_params("parallel", "arbitrary"),
    )(page_table, q, kn, vn, lam_vecs, gn, *([cache_k] * G), *([cache_v] * G))


def _gla_tables(c):
    t = np.arange(c)[:, None]
    u = np.arange(c)[None, :]
    mats = [u <= t, u > t]
    masks = []
    m = c // 2
    while m >= 1:
        mats.append((u > (t // m) * m) & (u <= t))
        mats.append((u > t) & (u <= (t // m + 1) * m))
        masks.append(((t // m) % 2 == 1) & ((u // m) == (t // m) - 1))
        m //= 2
    masks.append(t == u)
    return (np.concatenate(mats, 0).astype(np.float32), np.stack(masks, 0).astype(np.float32))


def _gla_kernel(q_ref, k_ref, v_ref, gd_ref, r_ref, s0_ref, wg_ref, bg_ref, gn_ref, mat_ref, mask_ref,
                o_ref, sout_ref, s_ref, *, c, valid, qscale):
    n = pl.program_id(1)
    heads, dk, dv = s_ref.shape

    @pl.when(n == 0)
    def _():
        s_ref[...] = s0_ref[0]

    mats = mat_ref[...]
    nlev = mask_ref.shape[0] - 1
    hs = range(heads)
    ks = [slice(h * dk, (h + 1) * dk) for h in hs]
    vs = [slice(h * dv, (h + 1) * dv) for h in hs]

    x = jnp.dot(gd_ref[...].astype(BF16), wg_ref[...], preferred_element_type=F32) + bg_ref[...]
    logg = (jnp.minimum(x, 0.0) - jnp.log1p(jnp.exp(-jnp.abs(x)))) * (1.0 / GATE_TAU)
    if valid < c:
        logg = jnp.where(lax.broadcasted_iota(jnp.int32, logg.shape, 0) < valid, logg, 0.0)
    hi = logg.astype(BF16)
    r1 = logg - hi.astype(F32)
    mid = r1.astype(BF16)
    lo = (r1 - mid.astype(F32)).astype(BF16)
    e = jnp.exp(jnp.dot(mats, hi, preferred_element_type=F32)
                + jnp.dot(mats, mid, preferred_element_type=F32)
                + jnp.dot(mats, lo, preferred_element_type=F32))

    q = q_ref[...] * qscale
    k = k_ref[...]
    v = v_ref[...].astype(BF16)
    qb = q.astype(BF16)
    kb = k.astype(BF16)
    q_in = (q * e[0:c]).astype(BF16)
    k_end = (k * e[c:2 * c]).astype(BF16)
    s_old = [s_ref[h] for h in hs]

    o = [jnp.dot(q_in[:, ks[h]], s_old[h].astype(BF16), preferred_element_type=F32) for h in hs]
    a = [mask_ref[nlev] * lax.dot_general(qb[:, ks[h]], kb[:, ks[h]], NT_DIMS, preferred_element_type=F32)
         for h in hs]
    for lv in range(nlev):
        qt = (q * e[(2 + 2 * lv) * c:(3 + 2 * lv) * c]).astype(BF16)
        kt = (k * e[(3 + 2 * lv) * c:(4 + 2 * lv) * c]).astype(BF16)
        a = [a[h] + mask_ref[lv] * lax.dot_general(qt[:, ks[h]], kt[:, ks[h]], NT_DIMS,
                                                   preferred_element_type=F32) for h in hs]
    o = [o[h] + jnp.dot(a[h].astype(BF16), v[:, vs[h]], preferred_element_type=F32) for h in hs]
    upd = [lax.dot_general(k_end[:, ks[h]], v[:, vs[h]], TN_DIMS, preferred_element_type=F32) for h in hs]

    decay_cols = jnp.broadcast_to(e[c - 1:c], (LANES, heads * dk)).T
    for h in hs:
        decay = jnp.concatenate([decay_cols[ks[h]]] * (dv // LANES), axis=1)
        s_ref[h] = decay * s_old[h] + upd[h]
        r = r_ref[:, vs[h]]
        o_ref[:, vs[h]] = (_rms(o[h], gn_ref[...]) * (r * jax.nn.sigmoid(r))).astype(o_ref.dtype)

    @pl.when(n == pl.num_programs(1) - 1)
    def _():
        sout_ref[0] = s_ref[...]


def _gla(z, zcols, gd, s0, batch0, wg, bg, gn, *, nb, seq, c, valid, out_dtype):
    _, heads, dk, dv = s0.shape
    rank = gd.shape[1]
    nc = seq // c
    wk, wv = heads * dk, heads * dv
    qc, kc, vc, rc = zcols
    assert qc % wk == 0 and kc % wk == 0 and vc % wv == 0 and rc % wv == 0
    mats, masks = _gla_tables(c)
    mats = jnp.asarray(mats, BF16)
    masks = jnp.asarray(masks, F32)

    def zspec(w, col0):
        return pl.BlockSpec((c, w), lambda b, n: (b * nc + n, col0 // w))

    return pl.pallas_call(
        functools.partial(_gla_kernel, c=c, valid=valid, qscale=dk ** -0.5),
        grid=(nb, nc),
        in_specs=[
            zspec(wk, qc), zspec(wk, kc), zspec(wv, vc),
            pl.BlockSpec((c, rank), lambda b, n: (b * nc + n, 0)),
            zspec(wv, rc),
            pl.BlockSpec((1, heads, dk, dv), lambda b, n: (batch0 + b, 0, 0, 0)),
            pl.BlockSpec((rank, wk), lambda b, n: (0, 0)),
            pl.BlockSpec((1, wk), lambda b, n: (0, 0)),
            pl.BlockSpec((1, dv), lambda b, n: (0, 0)),
            pl.BlockSpec(mats.shape, lambda b, n: (0, 0)),
            pl.BlockSpec(masks.shape, lambda b, n: (0, 0, 0)),
        ],
        out_specs=[
            pl.BlockSpec((c, wv), lambda b, n: (b * nc + n, 0)),
            pl.BlockSpec((1, heads, dk, dv), lambda b, n: (b, 0, 0, 0)),
        ],
        out_shape=[
            jax.ShapeDtypeStruct((nb * seq, wv), out_dtype),
            jax.ShapeDtypeStruct((nb, heads, dk, dv), F32),
        ],
        scratch_shapes=[pltpu.VMEM((heads, dk, dv), F32)],
        compiler_params=_params("parallel", "arbitrary"),
    )(z, z, z, gd, z, s0, wg, bg, gn, mats, masks)


def _oproj_kernel(oa_ref, og_ref, wa_ref, wg_ref, x_ref, y_ref):
    acc = jnp.dot(oa_ref[...], wa_ref[...], preferred_element_type=F32)
    acc = acc + jnp.dot(og_ref[...], wg_ref[...], preferred_element_type=F32)
    y_ref[...] = x_ref[...] + acc


def _oproj(oa, og, w, x, *, tm=1024, tn=512):
    n, d = x.shape
    wa_rows = oa.shape[1]
    wg_rows = og.shape[1]
    assert wa_rows == wg_rows
    tm = _tile(n, tm)
    tn = _tile(d, tn)
    return pl.pallas_call(
        _oproj_kernel,
        grid=(n // tm, d // tn),
        in_specs=[
            pl.BlockSpec((tm, wa_rows), lambda i, j: (i, 0)),
            pl.BlockSpec((tm, wg_rows), lambda i, j: (i, 0)),
            pl.BlockSpec((wa_rows, tn), lambda i, j: (0, j)),
            pl.BlockSpec((wg_rows, tn), lambda i, j: (1, j)),
            pl.BlockSpec((tm, tn), lambda i, j: (i, j)),
        ],
        out_specs=pl.BlockSpec((tm, tn), lambda i, j: (i, j)),
        out_shape=jax.ShapeDtypeStruct((n, d), F32),
        compiler_params=_params("parallel", "arbitrary"),
    )(oa, og, w, w, x)


def _row(v):
    return v.reshape(1, -1).astype(F32)


def kernel(x_prompt, x_sample, cache_k, cache_v, state_gla, page_table, w_in, attn_q_norm, attn_k_norm, lam_q1, lam_k1, lam_q2, lam_k2, attn_out_norm, gla_w_gate_up, gla_b_gate, gla_out_norm, w_out, ffn1_norm, ffn1_w_gate, ffn1_w_up, ffn1_w_down, mix_norm, ffn2_norm, ffn2_w_gate, ffn2_w_up, ffn2_w_down, final_norm):
    nbp, seq, d = x_prompt.shape
    nbs, ld, _ = x_sample.shape
    depth, n_pool, page, heads_a, w2 = cache_k.shape
    hd = w2 // 2
    vd = cache_v.shape[-1]
    _, _, heads_g, dk, dv = state_gla.shape
    rank = gla_w_gate_up.shape[1]
    past = page_table.shape[1] * page
    rot = hd // 4
    wq = heads_a * w2
    wv = heads_a * vd
    wgq = heads_g * dk
    wgv = heads_g * dv
    c_q, c_k, c_v = 0, wq, 2 * wq
    c_rest = 2 * wq + wv
    w_rest = 2 * wgq + 2 * wgv
    c_gd = c_rest + w_rest
    zcols = (0, wgq, 2 * wgq, 2 * wgq + wgv)

    tab_p = _rope_tables(jnp.arange(seq), hd, rot)
    tab_s = _rope_tables(past + jnp.arange(nbs * ld) % ld, hd, rot)
    chunk = math.gcd(seq, GLA_CHUNK)
    ld_pad = -(-ld // 8) * 8

    xp = x_prompt.reshape(nbp * seq, d)
    xs = x_sample.reshape(nbs * ld, d)
    cache_k4 = cache_k.reshape(depth * n_pool, page, heads_a, w2)
    cache_v4 = cache_v.reshape(depth * n_pool, page, heads_a, vd)
    state4 = state_gla.reshape(depth * nbs, heads_g, dk, dv)
    zero_state = jnp.zeros((nbp, heads_g, dk, dv), F32)
    outs = [[] for _ in range(6)]
    for l in range(depth):
        li = 0.8 - 0.6 * math.exp(-0.3 * l)
        ffn1_w = (ffn1_w_gate[l].astype(BF16), ffn1_w_up[l].astype(BF16), ffn1_w_down[l].astype(BF16))
        ffn2_w = (ffn2_w_gate[l].astype(BF16), ffn2_w_up[l].astype(BF16), ffn2_w_down[l].astype(BF16))
        w_in_b = w_in[l].astype(BF16)
        w_gd = w_in[l][:, c_gd:].astype(BF16)
        w_out_b = w_out[l].astype(BF16)
        lam_vecs = jnp.stack([lam_q1[l], lam_k1[l], lam_q2[l], lam_k2[l]]).astype(F32)
        gq, gk, gn_a = _row(attn_q_norm[l]), _row(attn_k_norm[l]), _row(attn_out_norm[l])
        wg = gla_w_gate_up[l].astype(BF16)
        bg, gn_g = _row(gla_b_gate[l]), _row(gla_out_norm[l])

        def dense_in(x, tables, q_dtype):
            x1, h = _ffn(x, _row(ffn1_norm[l]), *ffn1_w, _row(mix_norm[l]), final=False)
            (qa,) = _qk_proj(h, w_in_b, c_q, wq, gq, tables, [q_dtype], rot=rot)
            ka, ka_b = _qk_proj(h, w_in_b, c_k, wq, gk, tables, [F32, BF16], rot=rot)
            va, va_b = _proj(h, w_in_b, c_v, wv, [F32, BF16])
            (z,) = _proj(h, w_in_b, c_rest, w_rest, [F32])
            (gd,) = _proj(h, w_gd, 0, rank, [F32])
            return x1, qa, ka, ka_b, va, va_b, z, gd

        def dense_out(x1, oa, og):
            x2 = _oproj(oa, og, w_out_b, x1)
            return _ffn(x2, _row(ffn2_norm[l]), *ffn2_w, _row(final_norm[l]), final=True)

        x1, qa, ka, ka_b, va, va_b, z, gd = dense_in(xp, tab_p, BF16)
        oa = _prompt_attention(qa, ka_b, va_b, lam_vecs, gn_a, nb=nbp, seq=seq, hd=hd, li=li)
        og, s_p = _gla(z, zcols, gd, zero_state, 0, wg, bg, gn_g,
                       nb=nbp, seq=seq, c=chunk, valid=chunk, out_dtype=BF16)
        xp = dense_out(x1, oa, og)
        outs[0].append(ka.reshape(nbp, seq, heads_a, w2))
        outs[1].append(va.reshape(nbp, seq, heads_a, vd))
        outs[2].append(s_p)

        x1, qa, ka, ka_b, va, va_b, z, gd = dense_in(xs, tab_s, F32)
        oa = _sample_attention(qa.reshape(nbs, ld, wq), ka.reshape(nbs, ld, wq), va.reshape(nbs, ld, wv),
                               cache_k4, cache_v4, page_table, lam_vecs, gn_a, pool0=l * n_pool, hd=hd, li=li)
        pad = lambda a: jnp.pad(a.reshape(nbs, ld, -1), ((0, 0), (0, ld_pad - ld), (0, 0))).reshape(nbs * ld_pad, -1)
        og, s_s = _gla(pad(z), zcols, pad(gd), state4, l * nbs, wg, bg, gn_g,
                       nb=nbs, seq=ld_pad, c=ld_pad, valid=ld, out_dtype=F32)
        og = og.reshape(nbs, ld_pad, wgv)[:, :ld].reshape(nbs * ld, wgv)
        xs = dense_out(x1, oa.reshape(nbs * ld, wv).astype(BF16), og.astype(BF16))
        outs[3].append(ka.reshape(nbs, ld, heads_a, w2))
        outs[4].append(va.reshape(nbs, ld, heads_a, vd))
        outs[5].append(s_s)

    return (xp.reshape(nbp, seq, d), xs.reshape(nbs, ld, d), jnp.stack(outs[0]), jnp.stack(outs[1]),
            jnp.stack(outs[2]), jnp.stack(outs[3]), jnp.stack(outs[4]), jnp.stack(outs[5]))
```

```python
import functools
import math

import numpy as np
import jax
import jax.numpy as jnp
from jax import lax
from jax.experimental import pallas as pl
from jax.experimental.pallas import tpu as pltpu

F32 = jnp.float32
BF16 = jnp.bfloat16
EPS = 1e-6
ROPE_THETA = 500000.0
GATE_TAU = 16.0
FFN_ROWS = 512
GLA_CHUNK = 128
LANES = 128
MXU_COLS = 256
VMEM_LIMIT = 60 * 1024 * 1024
NT_DIMS = (((1,), (1,)), ((), ()))
TN_DIMS = (((0,), (0,)), ((), ()))


def _params(*sem):
    return pltpu.CompilerParams(dimension_semantics=sem, vmem_limit_bytes=VMEM_LIMIT)


def _rms(x, gain):
    ms = jnp.mean(x * x, axis=-1, keepdims=True)
    return x * lax.rsqrt(ms + EPS) * gain


def _tile(n, want):
    t = min(n, want)
    while n % t:
        t //= 2
    return t


def _ffn_kernel(x_ref, g_ref, wg_ref, wu_ref, wd_ref, pg_ref, *rest, final, emit_bf16):
    rest = list(rest)
    y_ref = rest.pop(0)
    h_ref = None if final else rest.pop(0)
    wb_refs = [rest.pop(0) for _ in range(3)] if emit_bf16 else None
    (xn_ref,) = rest
    j = pl.program_id(1)

    @pl.when(j == 0)
    def _():
        xn_ref[...] = _rms(x_ref[...], g_ref[...]).astype(BF16)
        y_ref[...] = jnp.zeros_like(y_ref)

    wg, wu, wd = wg_ref[...], wu_ref[...], wd_ref[...]
    if emit_bf16:
        wg, wu, wd = wg.astype(BF16), wu.astype(BF16), wd.astype(BF16)
        for ref, w in zip(wb_refs, (wg, wu, wd)):
            ref[...] = w
    xn = xn_ref[...]
    g = jnp.dot(xn, wg, preferred_element_type=F32)
    u = jnp.dot(xn, wu, preferred_element_type=F32)
    h = (g * jax.nn.sigmoid(g) * u).astype(BF16)
    y_ref[...] += jnp.dot(h, wd, preferred_element_type=F32)

    @pl.when(j == pl.num_programs(1) - 1)
    def _():
        y = x_ref[...] + 0.5 * y_ref[...]
        if final:
            y_ref[...] = _rms(y, pg_ref[...])
        else:
            y_ref[...] = y
            h_ref[...] = _rms(y, pg_ref[...]).astype(BF16)


def _ffn(x, norm_g, wg, wu, wd, post_g, *, final, tm=512, tf=256):
    n, d = x.shape
    dff = wd.shape[0]
    emit_bf16 = wd.dtype == F32
    if emit_bf16:
        tf //= 2
        assert n <= tm
    tf = _tile(dff, tf)
    nf = dff // tf
    tm = _tile(n, tm)
    out_shape = [jax.ShapeDtypeStruct((n, d), F32)]
    out_specs = [pl.BlockSpec((tm, d), lambda i, j: (i, 0))]
    if not final:
        out_shape.append(jax.ShapeDtypeStruct((n, d), BF16))
        out_specs.append(pl.BlockSpec((tm, d), lambda i, j: (i, 0)))
    if emit_bf16:
        out_shape += [jax.ShapeDtypeStruct(w.shape, BF16) for w in (wg, wu, wd)]
        out_specs += [pl.BlockSpec((d, tf), lambda i, j: (0, j)), pl.BlockSpec((d, tf), lambda i, j: (0, j)),
                      pl.BlockSpec((tf, d), lambda i, j: (j, 0))]
    res = pl.pallas_call(
        functools.partial(_ffn_kernel, final=final, emit_bf16=emit_bf16),
        grid=(n // tm, nf),
        in_specs=[
            pl.BlockSpec((tm, d), lambda i, j: (i, 0), pipeline_mode=pl.Buffered(1)),
            pl.BlockSpec((1, d), lambda i, j: (0, 0)),
            pl.BlockSpec((d, tf), lambda i, j: (0, j)),
            pl.BlockSpec((d, tf), lambda i, j: (0, j)),
            pl.BlockSpec((tf, d), lambda i, j: (j, 0)),
            pl.BlockSpec((1, d), lambda i, j: (0, 0)),
        ],
        out_specs=out_specs,
        out_shape=out_shape,
        scratch_shapes=[pltpu.VMEM((tm, d), BF16)],
        compiler_params=_params("parallel", "arbitrary"),
    )(x, norm_g, wg, wu, wd, post_g)
    n_main = 1 if final else 2
    main = res[0] if final else tuple(res[:2])
    return (main, tuple(res[n_main:])) if emit_bf16 else main


def _proj_kernel(h_ref, w_ref, *outs):
    acc = jnp.dot(h_ref[...], w_ref[...], preferred_element_type=F32)
    for o in outs:
        o[...] = acc.astype(o.dtype)


def _proj(h, w, col0, ncols, dtypes, *, tm=1024, tn=512):
    n, d = h.shape
    tm = _tile(n, tm)
    tn = _tile(ncols, tn)
    if ncols == w.shape[1]:
        off = 0
    else:
        assert col0 % tn == 0
        off = col0 // tn
    return pl.pallas_call(
        _proj_kernel,
        grid=(n // tm, ncols // tn),
        in_specs=[
            pl.BlockSpec((tm, d), lambda i, j: (i, 0)),
            pl.BlockSpec((d, tn), lambda i, j: (0, off + j)),
        ],
        out_specs=[pl.BlockSpec((tm, tn), lambda i, j: (i, j)) for _ in dtypes],
        out_shape=[jax.ShapeDtypeStruct((n, ncols), dt) for dt in dtypes],
        compiler_params=_params("parallel", "arbitrary"),
    )(h, w)


def _qk_kernel(h_ref, w_ref, gain_ref, c_ref, s_ref, perm_ref, *outs, tn, sub):
    h = h_ref[...]
    gain = gain_ref[...]
    reps = sub // LANES
    cos = jnp.concatenate([c_ref[...]] * reps, axis=1)
    sin = jnp.concatenate([s_ref[...]] * reps, axis=1)
    perm = perm_ref[...]
    acc = jnp.dot(h, w_ref[...], preferred_element_type=F32)
    for s0 in range(0, tn, sub):
        xn = jnp.concatenate([_rms(acc[:, c0:c0 + LANES], gain) for c0 in range(s0, s0 + sub, LANES)], axis=1)
        hi = xn.astype(BF16)
        lo = (xn - hi.astype(F32)).astype(BF16)
        partner = jnp.dot(jnp.concatenate([hi, lo], axis=1), perm, preferred_element_type=F32)
        y = xn * cos + partner * sin
        for o in outs:
            o[:, s0:s0 + sub] = y.astype(o.dtype)


def _rope_perm(hd, rot, sub):
    half = rot // 2
    p = np.zeros((hd, hd), np.float32)
    for l in range(half):
        p[l + half, l] = -1.0
        p[l, l + half] = 1.0
    bd = np.kron(np.eye(sub // hd, dtype=np.float32), p)
    return np.concatenate([bd, bd], axis=0)


def _qk_proj(h, w, col0, ncols, gain, tables, dtypes, *, rot, tm=1024, tn=512):
    n, d = h.shape
    tm = _tile(n, tm)
    tn = _tile(ncols, tn)
    cos, sin = tables
    hd = cos.shape[1]
    tm = _tile(cos.shape[0], tm)
    nt = cos.shape[0] // tm
    off = col0 // tn
    sub = min(tn, MXU_COLS)
    perm = jnp.asarray(_rope_perm(hd, rot, sub), BF16)
    tspec = pl.BlockSpec((tm, hd), lambda i, j: (i % nt, 0))
    return pl.pallas_call(
        functools.partial(_qk_kernel, tn=tn, sub=sub),
        grid=(n // tm, ncols // tn),
        in_specs=[
            pl.BlockSpec((tm, d), lambda i, j: (i, 0)),
            pl.BlockSpec((d, tn), lambda i, j: (0, off + j)),
            pl.BlockSpec((1, hd), lambda i, j: (0, 0)),
            tspec, tspec,
            pl.BlockSpec(perm.shape, lambda i, j: (0, 0)),
        ],
        out_specs=[pl.BlockSpec((tm, tn), lambda i, j: (i, j)) for _ in dtypes],
        out_shape=[jax.ShapeDtypeStruct((n, ncols), dt) for dt in dtypes],
        compiler_params=_params("parallel", "arbitrary"),
    )(h, w, gain, cos, sin, perm)


def _rope_tables(pos, hd, rot):
    half = rot // 2
    inv = ROPE_THETA ** (-jnp.arange(half, dtype=F32) * 2.0 / rot)
    ang = pos.astype(F32)[:, None] * inv[None, :]
    cos, sin = jnp.cos(ang), jnp.sin(ang)
    n = pos.shape[0]
    c = jnp.concatenate([cos, cos, jnp.ones((n, hd - rot), F32)], axis=1)
    s = jnp.concatenate([sin, sin, jnp.zeros((n, hd - rot), F32)], axis=1)
    return c, s


def _lambda(lam_ref, li):
    lv = lam_ref[...]
    a = jnp.sum(lv[0:1] * lv[1:2], axis=-1, keepdims=True)
    b = jnp.sum(lv[2:3] * lv[3:4], axis=-1, keepdims=True)
    return jnp.exp(a) - jnp.exp(b) + li


def _pattn_kernel(q_ref, k_ref, v_ref, lam_ref, gn_ref, o_ref, *, t, nq, hd, li):
    i = pl.program_id(2)
    c1 = hd ** -0.5 * math.log2(math.e)
    lam = _lambda(lam_ref, li)
    row = lax.broadcasted_iota(jnp.int32, (t, t), 0)
    col = lax.broadcasted_iota(jnp.int32, (t, t), 1)

    def tile(n):
        past = n * t
        q = q_ref[...]
        k = k_ref[0:past + t, :]
        e, inv = [], []
        for c in range(2):
            s = lax.dot_general(q[:, c * hd:(c + 1) * hd], k[:, c * hd:(c + 1) * hd], NT_DIMS,
                                preferred_element_type=F32)
            diag = jnp.where(row >= col, s[:, past:], -jnp.inf)
            s = jnp.concatenate([s[:, :past], diag], axis=1) if past else diag
            ec = jnp.exp2(s * c1 - jnp.max(s, axis=-1, keepdims=True) * c1)
            e.append(ec)
            inv.append(1.0 / jnp.sum(ec, axis=-1, keepdims=True))
        w = (e[0] * inv[0] - e[1] * (lam * inv[1])).astype(BF16)
        o = jnp.dot(w, v_ref[0:past + t, :], preferred_element_type=F32)
        o_ref[...] = (_rms(o, gn_ref[...]) * (1.0 - li)).astype(o_ref.dtype)

    for n in range(nq):
        pl.when(i == n)(functools.partial(tile, n))


def _prompt_attention(q, k, v, lam_vecs, gn, *, nb, seq, hd, li, t=256):
    n, width = q.shape
    heads = width // (2 * hd)
    vd = v.shape[1] // heads
    t = _tile(seq, t)
    nq = seq // t
    return pl.pallas_call(
        functools.partial(_pattn_kernel, t=t, nq=nq, hd=hd, li=li),
        grid=(nb, heads, nq),
        in_specs=[
            pl.BlockSpec((t, 2 * hd), lambda b, h, i: (b * nq + i, h)),
            pl.BlockSpec((seq, 2 * hd), lambda b, h, i: (b, h)),
            pl.BlockSpec((seq, vd), lambda b, h, i: (b, h)),
            pl.BlockSpec((4, hd), lambda b, h, i: (0, 0)),
            pl.BlockSpec((1, vd), lambda b, h, i: (0, 0)),
        ],
        out_specs=pl.BlockSpec((t, vd), lambda b, h, i: (b * nq + i, h)),
        out_shape=jax.ShapeDtypeStruct((n, heads * vd), BF16),
        compiler_params=_params("parallel", "parallel", "arbitrary"),
    )(q, k, v, lam_vecs, gn)


def _sattn_kernel(pt_ref, q_ref, kn_ref, vn_ref, lam_ref, gn_ref, *rest, G, heads, hd, ld, li):
    kp = rest[:G]
    vp = rest[G:2 * G]
    o_ref, qm_ref, m_ref, l_ref, acc_ref = rest[2 * G:]
    p = pl.program_id(1)
    scale = hd ** -0.5
    w2 = 2 * hd
    vd = acc_ref.shape[1]
    rows = 2 * ld
    page = kp[0].shape[1]
    pcols = page * heads

    @pl.when(p == 0)
    def _():
        row = lax.broadcasted_iota(jnp.int32, (rows, w2), 0)
        lane = lax.broadcasted_iota(jnp.int32, (rows, w2), 1)
        own_map = (row < ld) == (lane < hd)
        qi = lax.broadcasted_iota(jnp.int32, (rows, 1), 0) % ld
        qm, m0, l0, a0 = [], [], [], []
        for h in range(heads):
            qh = q_ref[0][:, h * w2:(h + 1) * w2]
            qf = jnp.where(own_map, jnp.concatenate([qh, qh], axis=0), 0.0)
            kn = kn_ref[0][:, h * w2:(h + 1) * w2]
            vn = vn_ref[0][:, h * vd:(h + 1) * vd]
            s = [jnp.where(qi >= j, jnp.sum(qf * kn[j:j + 1, :], axis=-1, keepdims=True) * scale, -jnp.inf)
                 for j in range(ld)]
            m = functools.reduce(jnp.maximum, s)
            pj = [jnp.exp(sj - m) for sj in s]
            qm.append(qf)
            m0.append(m)
            l0.append(functools.reduce(jnp.add, pj))
            a0.append(functools.reduce(jnp.add, [pj[j] * vn[j:j + 1, :] for j in range(ld)]))
        qm_ref[...] = jnp.concatenate(qm, axis=0).astype(BF16)
        m_ref[...] = jnp.concatenate(m0, axis=0)
        l_ref[...] = jnp.concatenate(l0, axis=0)
        acc_ref[...] = jnp.concatenate(a0, axis=0)

    nr = heads * rows
    qmat = qm_ref[...]
    same_head = (lax.broadcasted_iota(jnp.int32, (nr, pcols), 0) // rows
                 == lax.broadcasted_iota(jnp.int32, (nr, pcols), 1) % heads)
    s = jnp.concatenate(
        [jnp.where(same_head,
                   lax.dot_general(qmat, kp[g][0].reshape(pcols, w2).astype(BF16), NT_DIMS,
                                   preferred_element_type=F32) * scale, -jnp.inf)
         for g in range(G)], axis=1)
    m_old = m_ref[...]
    m_new = jnp.maximum(m_old, jnp.max(s, axis=-1, keepdims=True))
    alpha = jnp.exp(m_old - m_new)
    pr = jnp.exp(s - m_new)
    pv = functools.reduce(jnp.add, [
        jnp.dot(pr[:, g * pcols:(g + 1) * pcols].astype(BF16), vp[g][0].reshape(pcols, vd).astype(BF16),
                preferred_element_type=F32) for g in range(G)])
    m_ref[...] = m_new
    l_ref[...] = alpha * l_ref[...] + jnp.sum(pr, axis=-1, keepdims=True)
    acc_ref[...] = alpha * acc_ref[...] + pv

    @pl.when(p == pl.num_programs(1) - 1)
    def _():
        lam = _lambda(lam_ref, li)
        w = acc_ref[...] / l_ref[...]
        for h in range(heads):
            o = w[h * rows:h * rows + ld] - lam * w[h * rows + ld:(h + 1) * rows]
            o_ref[0, :, h * vd:(h + 1) * vd] = _rms(o, gn_ref[...]) * (1.0 - li)


def _sample_attention(q, kn, vn, cache_k, cache_v, page_table, lam_vecs, gn, *, pool0, hd, li, G=8):
    nb, ld, width = q.shape
    _, page, heads, w2 = cache_k.shape
    vwidth = vn.shape[2]
    vd = vwidth // heads
    n_pages = page_table.shape[1]
    G = _tile(n_pages, G)

    def page_spec(w, g):
        return pl.BlockSpec((1, page, heads, w), lambda b, p, pt: (pool0 + pt[b, p * G + g], 0, 0, 0))

    grid_spec = pltpu.PrefetchScalarGridSpec(
        num_scalar_prefetch=1,
        grid=(nb, n_pages // G),
        in_specs=[
            pl.BlockSpec((1, ld, width), lambda b, p, pt: (b, 0, 0)),
            pl.BlockSpec((1, ld, width), lambda b, p, pt: (b, 0, 0)),
            pl.BlockSpec((1, ld, vwidth), lambda b, p, pt: (b, 0, 0)),
            pl.BlockSpec((4, hd), lambda b, p, pt: (0, 0)),
            pl.BlockSpec((1, vd), lambda b, p, pt: (0, 0)),
        ] + [page_spec(w2, g) for g in range(G)] + [page_spec(vd, g) for g in range(G)],
        out_specs=pl.BlockSpec((1, ld, vwidth), lambda b, p, pt: (b, 0, 0)),
        scratch_shapes=[
            pltpu.VMEM((heads * 2 * ld, w2), BF16),
            pltpu.VMEM((heads * 2 * ld, 1), F32),
            pltpu.VMEM((heads * 2 * ld, 1), F32),
            pltpu.VMEM((heads * 2 * ld, vd), F32),
        ],
    )
    return pl.pallas_call(
        functools.partial(_sattn_kernel, G=G, heads=heads, hd=hd, ld=ld, li=li),
        grid_spec=grid_spec,
        out_shape=jax.ShapeDtypeStruct((nb, ld, vwidth), F32),
        compiler_params=_params("parallel", "arbitrary"),
    )(page_table, q, kn, vn, lam_vecs, gn, *([cache_k] * G), *([cache_v] * G))


def _gla_tables(c):
    t = np.arange(c)[:, None]
    u = np.arange(c)[None, :]
    mats = [u <= t, u > t]
    masks = []
    m = c // 2
    while m >= 1:
        mats.append((u > (t // m) * m) & (u <= t))
        mats.append((u > t) & (u <= (t // m + 1) * m))
        masks.append(((t // m) % 2 == 1) & ((u // m) == (t // m) - 1))
        m //= 2
    masks.append(t == u)
    return (np.concatenate(mats, 0).astype(np.float32), np.stack(masks, 0).astype(np.float32))


def _gla_kernel(q_ref, k_ref, v_ref, gd_ref, r_ref, s0_ref, wg_ref, bg_ref, gn_ref, mat_ref, mask_ref,
                o_ref, sout_ref, s_ref, *, c, valid, qscale):
    n = pl.program_id(1)
    heads, dk, dv = s_ref.shape

    @pl.when(n == 0)
    def _():
        s_ref[...] = s0_ref[0]

    mats = mat_ref[...]
    nlev = mask_ref.shape[0] - 1
    hs = range(heads)
    ks = [slice(h * dk, (h + 1) * dk) for h in hs]
    vs = [slice(h * dv, (h + 1) * dv) for h in hs]

    x = jnp.dot(gd_ref[...].astype(BF16), wg_ref[...], preferred_element_type=F32) + bg_ref[...]
    logg = (jnp.minimum(x, 0.0) - jnp.log1p(jnp.exp(-jnp.abs(x)))) * (1.0 / GATE_TAU)
    if valid < c:
        logg = jnp.where(lax.broadcasted_iota(jnp.int32, logg.shape, 0) < valid, logg, 0.0)
    hi = logg.astype(BF16)
    r1 = logg - hi.astype(F32)
    mid = r1.astype(BF16)
    lo = (r1 - mid.astype(F32)).astype(BF16)
    e = jnp.exp(jnp.dot(mats, hi, preferred_element_type=F32)
                + jnp.dot(mats, mid, preferred_element_type=F32)
                + jnp.dot(mats, lo, preferred_element_type=F32))

    q = q_ref[...] * qscale
    k = k_ref[...]
    v = v_ref[...].astype(BF16)
    qb = q.astype(BF16)
    kb = k.astype(BF16)
    q_in = (q * e[0:c]).astype(BF16)
    k_end = (k * e[c:2 * c]).astype(BF16)
    s_old = [s_ref[h] for h in hs]

    o = [jnp.dot(q_in[:, ks[h]], s_old[h].astype(BF16), preferred_element_type=F32) for h in hs]
    a = [mask_ref[nlev] * lax.dot_general(qb[:, ks[h]], kb[:, ks[h]], NT_DIMS, preferred_element_type=F32)
         for h in hs]
    for lv in range(nlev):
        qt = (q * e[(2 + 2 * lv) * c:(3 + 2 * lv) * c]).astype(BF16)
        kt = (k * e[(3 + 2 * lv) * c:(4 + 2 * lv) * c]).astype(BF16)
        a = [a[h] + mask_ref[lv] * lax.dot_general(qt[:, ks[h]], kt[:, ks[h]], NT_DIMS,
                                                   preferred_element_type=F32) for h in hs]
    o = [o[h] + jnp.dot(a[h].astype(BF16), v[:, vs[h]], preferred_element_type=F32) for h in hs]
    upd = [lax.dot_general(k_end[:, ks[h]], v[:, vs[h]], TN_DIMS, preferred_element_type=F32) for h in hs]

    decay_cols = jnp.broadcast_to(e[c - 1:c], (LANES, heads * dk)).T
    for h in hs:
        decay = jnp.concatenate([decay_cols[ks[h]]] * (dv // LANES), axis=1)
        s_ref[h] = decay * s_old[h] + upd[h]
        r = r_ref[:, vs[h]]
        o_ref[:, vs[h]] = (_rms(o[h], gn_ref[...]) * (r * jax.nn.sigmoid(r))).astype(o_ref.dtype)

    @pl.when(n == pl.num_programs(1) - 1)
    def _():
        sout_ref[0] = s_ref[...]


def _gla(z, zcols, gd, s0, batch0, wg, bg, gn, *, nb, seq, c, valid, out_dtype):
    _, heads, dk, dv = s0.shape
    rank = gd.shape[1]
    nc = seq // c
    wk, wv = heads * dk, heads * dv
    qc, kc, vc, rc = zcols
    assert qc % wk == 0 and kc % wk == 0 and vc % wv == 0 and rc % wv == 0
    mats, masks = _gla_tables(c)
    mats = jnp.asarray(mats, BF16)
    masks = jnp.asarray(masks, F32)

    def zspec(w, col0):
        return pl.BlockSpec((c, w), lambda b, n: (b * nc + n, col0 // w))

    return pl.pallas_call(
        functools.partial(_gla_kernel, c=c, valid=valid, qscale=dk ** -0.5),
        grid=(nb, nc),
        in_specs=[
            zspec(wk, qc), zspec(wk, kc), zspec(wv, vc),
            pl.BlockSpec((c, rank), lambda b, n: (b * nc + n, 0)),
            zspec(wv, rc),
            pl.BlockSpec((1, heads, dk, dv), lambda b, n: (batch0 + b, 0, 0, 0)),
            pl.BlockSpec((rank, wk), lambda b, n: (0, 0)),
            pl.BlockSpec((1, wk), lambda b, n: (0, 0)),
            pl.BlockSpec((1, dv), lambda b, n: (0, 0)),
            pl.BlockSpec(mats.shape, lambda b, n: (0, 0)),
            pl.BlockSpec(masks.shape, lambda b, n: (0, 0, 0)),
        ],
        out_specs=[
            pl.BlockSpec((c, wv), lambda b, n: (b * nc + n, 0)),
            pl.BlockSpec((1, heads, dk, dv), lambda b, n: (b, 0, 0, 0)),
        ],
        out_shape=[
            jax.ShapeDtypeStruct((nb * seq, wv), out_dtype),
            jax.ShapeDtypeStruct((nb, heads, dk, dv), F32),
        ],
        scratch_shapes=[pltpu.VMEM((heads, dk, dv), F32)],
        compiler_params=_params("parallel", "arbitrary"),
    )(z, z, z, gd, z, s0, wg, bg, gn, mats, masks)


def _oproj_kernel(oa_ref, og_ref, wa_ref, wg_ref, x_ref, y_ref):
    acc = jnp.dot(oa_ref[...], wa_ref[...], preferred_element_type=F32)
    acc = acc + jnp.dot(og_ref[...], wg_ref[...], preferred_element_type=F32)
    y_ref[...] = x_ref[...] + acc


def _oproj(oa, og, w, x, *, tm=1024, tn=512):
    n, d = x.shape
    wa_rows = oa.shape[1]
    wg_rows = og.shape[1]
    assert wa_rows == wg_rows
    tm = _tile(n, tm)
    tn = _tile(d, tn)
    return pl.pallas_call(
        _oproj_kernel,
        grid=(n // tm, d // tn),
        in_specs=[
            pl.BlockSpec((tm, wa_rows), lambda i, j: (i, 0)),
            pl.BlockSpec((tm, wg_rows), lambda i, j: (i, 0)),
            pl.BlockSpec((wa_rows, tn), lambda i, j: (0, j)),
            pl.BlockSpec((wg_rows, tn), lambda i, j: (1, j)),
            pl.BlockSpec((tm, tn), lambda i, j: (i, j)),
        ],
        out_specs=pl.BlockSpec((tm, tn), lambda i, j: (i, j)),
        out_shape=jax.ShapeDtypeStruct((n, d), F32),
        compiler_params=_params("parallel", "arbitrary"),
    )(oa, og, w, w, x)


def _row(v):
    return v.reshape(1, -1).astype(F32)


def kernel(x_prompt, x_sample, cache_k, cache_v, state_gla, page_table, w_in, attn_q_norm, attn_k_norm, lam_q1, lam_k1, lam_q2, lam_k2, attn_out_norm, gla_w_gate_up, gla_b_gate, gla_out_norm, w_out, ffn1_norm, ffn1_w_gate, ffn1_w_up, ffn1_w_down, mix_norm, ffn2_norm, ffn2_w_gate, ffn2_w_up, ffn2_w_down, final_norm):
    nbp, seq, d = x_prompt.shape
    nbs, ld, _ = x_sample.shape
    depth, n_pool, page, heads_a, w2 = cache_k.shape
    hd = w2 // 2
    vd = cache_v.shape[-1]
    _, _, heads_g, dk, dv = state_gla.shape
    rank = gla_w_gate_up.shape[1]
    past = page_table.shape[1] * page
    rot = hd // 4
    wq = heads_a * w2
    wv = heads_a * vd
    wgq = heads_g * dk
    wgv = heads_g * dv
    c_q, c_k, c_v = 0, wq, 2 * wq
    c_rest = 2 * wq + wv
    w_rest = 2 * wgq + 2 * wgv
    c_gd = c_rest + w_rest
    zcols = (0, wgq, 2 * wgq, 2 * wgq + wgv)

    tab_p = _rope_tables(jnp.arange(seq), hd, rot)
    tab_s = _rope_tables(past + jnp.arange(nbs * ld) % ld, hd, rot)
    chunk = math.gcd(seq, GLA_CHUNK)
    ld_pad = -(-ld // 8) * 8

    xp = x_prompt.reshape(nbp * seq, d)
    xs = x_sample.reshape(nbs * ld, d)
    cache_k4 = cache_k.reshape(depth * n_pool, page, heads_a, w2)
    cache_v4 = cache_v.reshape(depth * n_pool, page, heads_a, vd)
    state4 = state_gla.reshape(depth * nbs, heads_g, dk, dv)
    zero_state = jnp.zeros((nbp, heads_g, dk, dv), F32)
    outs = [[] for _ in range(6)]
    for l in range(depth):
        li = 0.8 - 0.6 * math.exp(-0.3 * l)
        ffn1_w = (ffn1_w_gate[l], ffn1_w_up[l], ffn1_w_down[l])
        ffn2_w = (ffn2_w_gate[l], ffn2_w_up[l], ffn2_w_down[l])
        if nbs * ld > FFN_ROWS:
            ffn1_w = tuple(w.astype(BF16) for w in ffn1_w)
            ffn2_w = tuple(w.astype(BF16) for w in ffn2_w)
        w_in_b = w_in[l].astype(BF16)
        w_gd = w_in[l][:, c_gd:].astype(BF16)
        w_out_b = w_out[l].astype(BF16)
        lam_vecs = jnp.stack([lam_q1[l], lam_k1[l], lam_q2[l], lam_k2[l]]).astype(F32)
        gq, gk, gn_a = _row(attn_q_norm[l]), _row(attn_k_norm[l]), _row(attn_out_norm[l])
        wg = gla_w_gate_up[l].astype(BF16)
        bg, gn_g = _row(gla_b_gate[l]), _row(gla_out_norm[l])

        def ffn(x, norm, w, post, final):
            res = _ffn(x, _row(norm), *w, _row(post), final=final, tm=FFN_ROWS)
            return res if w[0].dtype == F32 else (res, w)

        def dense_in(x, tables, q_dtype, w):
            (x1, h), w = ffn(x, ffn1_norm[l], w, mix_norm[l], False)
            (qa,) = _qk_proj(h, w_in_b, c_q, wq, gq, tables, [q_dtype], rot=rot)
            ka, ka_b = _qk_proj(h, w_in_b, c_k, wq, gk, tables, [F32, BF16], rot=rot)
            va, va_b = _proj(h, w_in_b, c_v, wv, [F32, BF16])
            (z,) = _proj(h, w_in_b, c_rest, w_rest, [F32])
            (gd,) = _proj(h, w_gd, 0, rank, [F32])
            return (x1, qa, ka, ka_b, va, va_b, z, gd), w

        def dense_out(x1, oa, og, w):
            x2 = _oproj(oa, og, w_out_b, x1)
            return ffn(x2, ffn2_norm[l], w, final_norm[l], True)

        (x1, qa, ka, ka_b, va, va_b, z, gd), ffn1_w = dense_in(xs, tab_s, F32, ffn1_w)
        oa = _sample_attention(qa.reshape(nbs, ld, wq), ka.reshape(nbs, ld, wq), va.reshape(nbs, ld, wv),
                               cache_k4, cache_v4, page_table, lam_vecs, gn_a, pool0=l * n_pool, hd=hd, li=li)
        pad = lambda a: jnp.pad(a.reshape(nbs, ld, -1), ((0, 0), (0, ld_pad - ld), (0, 0))).reshape(nbs * ld_pad, -1)
        og, s_s = _gla(pad(z), zcols, pad(gd), state4, l * nbs, wg, bg, gn_g,
                       nb=nbs, seq=ld_pad, c=ld_pad, valid=ld, out_dtype=F32)
        og = og.reshape(nbs, ld_pad, wgv)[:, :ld].reshape(nbs * ld, wgv)
        xs, ffn2_w = dense_out(x1, oa.reshape(nbs * ld, wv).astype(BF16), og.astype(BF16), ffn2_w)
        outs[3].append(ka.reshape(nbs, ld, heads_a, w2))
        outs[4].append(va.reshape(nbs, ld, heads_a, vd))
        outs[5].append(s_s)

        (x1, qa, ka, ka_b, va, va_b, z, gd), _ = dense_in(xp, tab_p, BF16, ffn1_w)
        oa = _prompt_attention(qa, ka_b, va_b, lam_vecs, gn_a, nb=nbp, seq=seq, hd=hd, li=li)
        og, s_p = _gla(z, zcols, gd, zero_state, 0, wg, bg, gn_g,
                       nb=nbp, seq=seq, c=chunk, valid=chunk, out_dtype=BF16)
        xp, _ = dense_out(x1, oa, og, ffn2_w)
        outs[0].append(ka.reshape(nbp, seq, heads_a, w2))
        outs[1].append(va.reshape(nbp, seq, heads_a, vd))
        outs[2].append(s_p)

    return (xp.reshape(nbp, seq, d), xs.reshape(nbs, ld, d), jnp.stack(outs[0]), jnp.stack(outs[1]),
            jnp.stack(outs[2]), jnp.stack(outs[3]), jnp.stack(outs[4]), jnp.stack(outs[5]))
```

```python
import functools
import math

import numpy as np
import jax
import jax.numpy as jnp
from jax import lax
from jax.experimental import pallas as pl
from jax.experimental.pallas import tpu as pltpu

F32 = jnp.float32
BF16 = jnp.bfloat16
EPS = 1e-6
ROPE_THETA = 500000.0
GATE_TAU = 16.0
FFN_ROWS = 512
GLA_CHUNK = 128
LANES = 128
MXU_COLS = 256
VMEM_LIMIT = 60 * 1024 * 1024
NT_DIMS = (((1,), (1,)), ((), ()))
TN_DIMS = (((0,), (0,)), ((), ()))


def _params(*sem):
    return pltpu.CompilerParams(dimension_semantics=sem, vmem_limit_bytes=VMEM_LIMIT)


def _rms(x, gain):
    ms = jnp.mean(x * x, axis=-1, keepdims=True)
    return x * lax.rsqrt(ms + EPS) * gain


def _tile(n, want):
    t = min(n, want)
    while n % t:
        t //= 2
    return t


def _ffn_kernel(x_ref, g_ref, wg_ref, wu_ref, wd_ref, pg_ref, *rest, final, emit_bf16, tail):
    rest = list(rest)
    y_ref = rest.pop(0)
    wb_refs = [rest.pop(0) for _ in range(3)] if emit_bf16 else None
    (xn_ref,) = rest
    j = pl.program_id(1)
    last = pl.num_programs(1) - 1
    tf = wd_ref.shape[0]

    @pl.when(j == 0)
    def _():
        xn_ref[...] = _rms(x_ref[...], g_ref[...]).astype(BF16)
        y_ref[...] = jnp.zeros_like(y_ref)

    def accumulate(width):
        wg, wu, wd = wg_ref[:, :width], wu_ref[:, :width], wd_ref[:width, :]
        if emit_bf16:
            wg, wu, wd = wg.astype(BF16), wu.astype(BF16), wd.astype(BF16)
            for ref, w in zip(wb_refs, (wg, wu, wd)):
                ref[...] = w
        xn = xn_ref[...]
        g = jnp.dot(xn, wg, preferred_element_type=F32)
        u = jnp.dot(xn, wu, preferred_element_type=F32)
        h = (g * jax.nn.sigmoid(g) * u).astype(BF16)
        y_ref[...] += jnp.dot(h, wd, preferred_element_type=F32)

    if tail == tf:
        accumulate(tf)
    else:
        pl.when(j < last)(functools.partial(accumulate, tf))
        pl.when(j == last)(functools.partial(accumulate, tail))

    @pl.when(j == last)
    def _():
        y = x_ref[...] + 0.5 * y_ref[...]
        y_ref[...] = _rms(y, pg_ref[...]) if final else y


def _ffn(x, norm_g, wg, wu, wd, post_g, *, final, tm=FFN_ROWS, tf=512):
    n, d = x.shape
    dff = wd.shape[0]
    emit_bf16 = wd.dtype == F32
    if emit_bf16:
        tf = _tile(dff, tf // 4)
        assert n <= tm
    nf = pl.cdiv(dff, tf)
    tail = dff - (nf - 1) * tf
    tm = _tile(n, tm)
    out_shape = [jax.ShapeDtypeStruct((n, d), F32)]
    out_specs = [pl.BlockSpec((tm, d), lambda i, j: (i, 0))]
    if emit_bf16:
        out_shape += [jax.ShapeDtypeStruct(w.shape, BF16) for w in (wg, wu, wd)]
        out_specs += [pl.BlockSpec((d, tf), lambda i, j: (0, j)), pl.BlockSpec((d, tf), lambda i, j: (0, j)),
                      pl.BlockSpec((tf, d), lambda i, j: (j, 0))]
    res = pl.pallas_call(
        functools.partial(_ffn_kernel, final=final, emit_bf16=emit_bf16, tail=tail),
        grid=(n // tm, nf),
        in_specs=[
            pl.BlockSpec((tm, d), lambda i, j: (i, 0), pipeline_mode=pl.Buffered(1)),
            pl.BlockSpec((1, d), lambda i, j: (0, 0)),
            pl.BlockSpec((d, tf), lambda i, j: (0, j)),
            pl.BlockSpec((d, tf), lambda i, j: (0, j)),
            pl.BlockSpec((tf, d), lambda i, j: (j, 0)),
            pl.BlockSpec((1, d), lambda i, j: (0, 0)),
        ],
        out_specs=out_specs,
        out_shape=out_shape,
        scratch_shapes=[pltpu.VMEM((tm, d), BF16)],
        compiler_params=_params("parallel", "arbitrary"),
    )(x, norm_g, wg, wu, wd, post_g)
    return (res[0], tuple(res[1:])) if emit_bf16 else res[0]


def _norm_kernel(x_ref, g_ref, o_ref):
    o_ref[...] = _rms(x_ref[...], g_ref[...]).astype(o_ref.dtype)


def _norm(x, gain, *, tm=FFN_ROWS):
    n, d = x.shape
    tm = _tile(n, tm)
    return pl.pallas_call(
        _norm_kernel,
        grid=(n // tm,),
        in_specs=[pl.BlockSpec((tm, d), lambda i: (i, 0)), pl.BlockSpec((1, d), lambda i: (0, 0))],
        out_specs=pl.BlockSpec((tm, d), lambda i: (i, 0)),
        out_shape=jax.ShapeDtypeStruct((n, d), BF16),
        compiler_params=_params("parallel"),
    )(x, gain)


def _weight_block(w_ref, outs):
    w = w_ref[...]
    if w.dtype == BF16:
        return w, outs
    w = w.astype(BF16)
    outs[-1][...] = w
    return w, outs[:-1]


def _weight_specs(w, n, tm, d, tn, ncols, out_specs, out_shape):
    if w.dtype == BF16:
        return out_specs, out_shape
    assert n <= tm
    return (out_specs + [pl.BlockSpec((d, tn), lambda i, j: (0, j))],
            out_shape + [jax.ShapeDtypeStruct((d, ncols), BF16)])


def _proj_kernel(h_ref, w_ref, *outs):
    w, outs = _weight_block(w_ref, outs)
    acc = jnp.dot(h_ref[...], w, preferred_element_type=F32)
    for o in outs:
        o[...] = acc.astype(o.dtype)


def _proj(h, w, col0, ncols, dtypes, *, tm=1024, tn=512):
    n, d = h.shape
    tm = _tile(n, tm)
    tn = _tile(ncols, tn)
    if ncols == w.shape[1]:
        off = 0
    else:
        assert col0 % tn == 0
        off = col0 // tn
    out_specs, out_shape = _weight_specs(
        w, n, tm, d, tn, ncols,
        [pl.BlockSpec((tm, tn), lambda i, j: (i, j)) for _ in dtypes],
        [jax.ShapeDtypeStruct((n, ncols), dt) for dt in dtypes])
    return pl.pallas_call(
        _proj_kernel,
        grid=(n // tm, ncols // tn),
        in_specs=[
            pl.BlockSpec((tm, d), lambda i, j: (i, 0)),
            pl.BlockSpec((d, tn), lambda i, j: (0, off + j)),
        ],
        out_specs=out_specs,
        out_shape=out_shape,
        compiler_params=_params("parallel", "arbitrary"),
    )(h, w)


def _qk_kernel(h_ref, w_ref, gain_ref, c_ref, s_ref, perm_ref, *outs, tn, sub):
    h = h_ref[...]
    gain = gain_ref[...]
    reps = sub // LANES
    cos = jnp.concatenate([c_ref[...]] * reps, axis=1)
    sin = jnp.concatenate([s_ref[...]] * reps, axis=1)
    perm = perm_ref[...]
    w, outs = _weight_block(w_ref, outs)
    acc = jnp.dot(h, w, preferred_element_type=F32)
    for s0 in range(0, tn, sub):
        xn = jnp.concatenate([_rms(acc[:, c0:c0 + LANES], gain) for c0 in range(s0, s0 + sub, LANES)], axis=1)
        hi = xn.astype(BF16)
        lo = (xn - hi.astype(F32)).astype(BF16)
        partner = jnp.dot(jnp.concatenate([hi, lo], axis=1), perm, preferred_element_type=F32)
        y = xn * cos + partner * sin
        for o in outs:
            o[:, s0:s0 + sub] = y.astype(o.dtype)


def _rope_perm(hd, rot, sub):
    half = rot // 2
    p = np.zeros((hd, hd), np.float32)
    for l in range(half):
        p[l + half, l] = -1.0
        p[l, l + half] = 1.0
    bd = np.kron(np.eye(sub // hd, dtype=np.float32), p)
    return np.concatenate([bd, bd], axis=0)


def _qk_proj(h, w, col0, ncols, gain, tables, dtypes, *, rot, tm=1024, tn=512):
    n, d = h.shape
    tm = _tile(n, tm)
    tn = _tile(ncols, tn)
    cos, sin = tables
    hd = cos.shape[1]
    tm = _tile(cos.shape[0], tm)
    nt = cos.shape[0] // tm
    off = col0 // tn
    sub = min(tn, MXU_COLS)
    perm = jnp.asarray(_rope_perm(hd, rot, sub), BF16)
    tspec = pl.BlockSpec((tm, hd), lambda i, j: (i % nt, 0))
    out_specs, out_shape = _weight_specs(
        w, n, tm, d, tn, ncols,
        [pl.BlockSpec((tm, tn), lambda i, j: (i, j)) for _ in dtypes],
        [jax.ShapeDtypeStruct((n, ncols), dt) for dt in dtypes])
    return pl.pallas_call(
        functools.partial(_qk_kernel, tn=tn, sub=sub),
        grid=(n // tm, ncols // tn),
        in_specs=[
            pl.BlockSpec((tm, d), lambda i, j: (i, 0)),
            pl.BlockSpec((d, tn), lambda i, j: (0, off + j)),
            pl.BlockSpec((1, hd), lambda i, j: (0, 0)),
            tspec, tspec,
            pl.BlockSpec(perm.shape, lambda i, j: (0, 0)),
        ],
        out_specs=out_specs,
        out_shape=out_shape,
        compiler_params=_params("parallel", "arbitrary"),
    )(h, w, gain, cos, sin, perm)


def _rope_tables(pos, hd, rot):
    half = rot // 2
    inv = ROPE_THETA ** (-jnp.arange(half, dtype=F32) * 2.0 / rot)
    ang = pos.astype(F32)[:, None] * inv[None, :]
    cos, sin = jnp.cos(ang), jnp.sin(ang)
    n = pos.shape[0]
    c = jnp.concatenate([cos, cos, jnp.ones((n, hd - rot), F32)], axis=1)
    s = jnp.concatenate([sin, sin, jnp.zeros((n, hd - rot), F32)], axis=1)
    return c, s


def _lambda(lam_ref, li):
    lv = lam_ref[...]
    a = jnp.sum(lv[0:1] * lv[1:2], axis=-1, keepdims=True)
    b = jnp.sum(lv[2:3] * lv[3:4], axis=-1, keepdims=True)
    return jnp.exp(a) - jnp.exp(b) + li


def _pattn_kernel(q_ref, k_ref, v_ref, lam_ref, gn_ref, o_ref, *, t, nq, hd, li):
    c1 = hd ** -0.5 * math.log2(math.e)
    lam = _lambda(lam_ref, li)
    row = lax.broadcasted_iota(jnp.int32, (t, t), 0)
    col = lax.broadcasted_iota(jnp.int32, (t, t), 1)

    for n in range(nq):
        past = n * t
        q = q_ref[past:past + t, :]
        k = k_ref[0:past + t, :]
        e, inv = [], []
        for c in range(2):
            s = lax.dot_general(q[:, c * hd:(c + 1) * hd], k[:, c * hd:(c + 1) * hd], NT_DIMS,
                                preferred_element_type=F32)
            diag = jnp.where(row >= col, s[:, past:], -jnp.inf)
            s = jnp.concatenate([s[:, :past], diag], axis=1) if past else diag
            ec = jnp.exp2(s * c1 - jnp.max(s, axis=-1, keepdims=True) * c1)
            e.append(ec)
            inv.append(1.0 / jnp.sum(ec, axis=-1, keepdims=True))
        w = (e[0] * inv[0] - e[1] * (lam * inv[1])).astype(BF16)
        o = jnp.dot(w, v_ref[0:past + t, :], preferred_element_type=F32)
        o_ref[past:past + t, :] = (_rms(o, gn_ref[...]) * (1.0 - li)).astype(o_ref.dtype)


def _prompt_attention(q, k, v, lam_vecs, gn, *, nb, seq, hd, li, t=256):
    n, width = q.shape
    heads = width // (2 * hd)
    vd = v.shape[1] // heads
    t = _tile(seq, t)
    return pl.pallas_call(
        functools.partial(_pattn_kernel, t=t, nq=seq // t, hd=hd, li=li),
        grid=(nb, heads),
        in_specs=[
            pl.BlockSpec((seq, 2 * hd), lambda b, h: (b, h)),
            pl.BlockSpec((seq, 2 * hd), lambda b, h: (b, h)),
            pl.BlockSpec((seq, vd), lambda b, h: (b, h)),
            pl.BlockSpec((4, hd), lambda b, h: (0, 0)),
            pl.BlockSpec((1, vd), lambda b, h: (0, 0)),
        ],
        out_specs=pl.BlockSpec((seq, vd), lambda b, h: (b, h)),
        out_shape=jax.ShapeDtypeStruct((n, heads * vd), BF16),
        compiler_params=_params("parallel", "parallel"),
    )(q, k, v, lam_vecs, gn)


def _sattn_kernel(pt_ref, q_ref, kn_ref, vn_ref, lam_ref, gn_ref, *rest, G, heads, hd, ld, li):
    kp = rest[:G]
    vp = rest[G:2 * G]
    o_ref, qm_ref, m_ref, l_ref, acc_ref = rest[2 * G:]
    p = pl.program_id(1)
    scale = hd ** -0.5
    w2 = 2 * hd
    vd = acc_ref.shape[1]
    rows = 2 * ld
    page = kp[0].shape[1]
    pcols = page * heads

    @pl.when(p == 0)
    def _():
        row = lax.broadcasted_iota(jnp.int32, (rows, w2), 0)
        lane = lax.broadcasted_iota(jnp.int32, (rows, w2), 1)
        own_map = (row < ld) == (lane < hd)
        qi = lax.broadcasted_iota(jnp.int32, (rows, 1), 0) % ld
        qm, m0, l0, a0 = [], [], [], []
        for h in range(heads):
            qh = q_ref[0][:, h * w2:(h + 1) * w2]
            qf = jnp.where(own_map, jnp.concatenate([qh, qh], axis=0), 0.0)
            kn = kn_ref[0][:, h * w2:(h + 1) * w2]
            vn = vn_ref[0][:, h * vd:(h + 1) * vd]
            s = [jnp.where(qi >= j, jnp.sum(qf * kn[j:j + 1, :], axis=-1, keepdims=True) * scale, -jnp.inf)
                 for j in range(ld)]
            m = functools.reduce(jnp.maximum, s)
            pj = [jnp.exp(sj - m) for sj in s]
            qm.append(qf)
            m0.append(m)
            l0.append(functools.reduce(jnp.add, pj))
            a0.append(functools.reduce(jnp.add, [pj[j] * vn[j:j + 1, :] for j in range(ld)]))
        qm_ref[...] = jnp.concatenate(qm, axis=0).astype(BF16)
        m_ref[...] = jnp.concatenate(m0, axis=0)
        l_ref[...] = jnp.concatenate(l0, axis=0)
        acc_ref[...] = jnp.concatenate(a0, axis=0)

    nr = heads * rows
    qmat = qm_ref[...]
    same_head = (lax.broadcasted_iota(jnp.int32, (nr, pcols), 0) // rows
                 == lax.broadcasted_iota(jnp.int32, (nr, pcols), 1) % heads)
    s = jnp.concatenate(
        [jnp.where(same_head,
                   lax.dot_general(qmat, kp[g][0].reshape(pcols, w2).astype(BF16), NT_DIMS,
                                   preferred_element_type=F32) * scale, -jnp.inf)
         for g in range(G)], axis=1)
    m_old = m_ref[...]
    m_new = jnp.maximum(m_old, jnp.max(s, axis=-1, keepdims=True))
    alpha = jnp.exp(m_old - m_new)
    pr = jnp.exp(s - m_new)
    pv = functools.reduce(jnp.add, [
        jnp.dot(pr[:, g * pcols:(g + 1) * pcols].astype(BF16), vp[g][0].reshape(pcols, vd).astype(BF16),
                preferred_element_type=F32) for g in range(G)])
    m_ref[...] = m_new
    l_ref[...] = alpha * l_ref[...] + jnp.sum(pr, axis=-1, keepdims=True)
    acc_ref[...] = alpha * acc_ref[...] + pv

    @pl.when(p == pl.num_programs(1) - 1)
    def _():
        lam = _lambda(lam_ref, li)
        w = acc_ref[...] / l_ref[...]
        for h in range(heads):
            o = w[h * rows:h * rows + ld] - lam * w[h * rows + ld:(h + 1) * rows]
            o_ref[0, :, h * vd:(h + 1) * vd] = _rms(o, gn_ref[...]) * (1.0 - li)


def _sample_attention(q, kn, vn, cache_k, cache_v, page_table, lam_vecs, gn, *, pool0, hd, li, G=8):
    nb, ld, width = q.shape
    _, page, heads, w2 = cache_k.shape
    vwidth = vn.shape[2]
    vd = vwidth // heads
    n_pages = page_table.shape[1]
    G = _tile(n_pages, G)

    def page_spec(w, g):
        return pl.BlockSpec((1, page, heads, w), lambda b, p, pt: (pool0 + pt[b, p * G + g], 0, 0, 0))

    grid_spec = pltpu.PrefetchScalarGridSpec(
        num_scalar_prefetch=1,
        grid=(nb, n_pages // G),
        in_specs=[
            pl.BlockSpec((1, ld, width), lambda b, p, pt: (b, 0, 0)),
            pl.BlockSpec((1, ld, width), lambda b, p, pt: (b, 0, 0)),
            pl.BlockSpec((1, ld, vwidth), lambda b, p, pt: (b, 0, 0)),
            pl.BlockSpec((4, hd), lambda b, p, pt: (0, 0)),
            pl.BlockSpec((1, vd), lambda b, p, pt: (0, 0)),
        ] + [page_spec(w2, g) for g in range(G)] + [page_spec(vd, g) for g in range(G)],
        out_specs=pl.BlockSpec((1, ld, vwidth), lambda b, p, pt: (b, 0, 0)),
        scratch_shapes=[
            pltpu.VMEM((heads * 2 * ld, w2), BF16),
            pltpu.VMEM((heads * 2 * ld, 1), F32),
            pltpu.VMEM((heads * 2 * ld, 1), F32),
            pltpu.VMEM((heads * 2 * ld, vd), F32),
        ],
    )
    return pl.pallas_call(
        functools.partial(_sattn_kernel, G=G, heads=heads, hd=hd, ld=ld, li=li),
        grid_spec=grid_spec,
        out_shape=jax.ShapeDtypeStruct((nb, ld, vwidth), F32),
        compiler_params=_params("parallel", "arbitrary"),
    )(page_table, q, kn, vn, lam_vecs, gn, *([cache_k] * G), *([cache_v] * G))


def _gla_tables(c):
    t = np.arange(c)[:, None]
    u = np.arange(c)[None, :]
    mats = [u <= t, u > t]
    masks = []
    m = c // 2
    while m >= 1:
        mats.append((u > (t // m) * m) & (u <= t))
        mats.append((u > t) & (u <= (t // m + 1) * m))
        masks.append(((t // m) % 2 == 1) & ((u // m) == (t // m) - 1))
        m //= 2
    masks.append(t == u)
    return (np.concatenate(mats, 0).astype(np.float32), np.stack(masks, 0).astype(np.float32))


def _gla_kernel(q_ref, k_ref, v_ref, gd_ref, r_ref, s0_ref, wg_ref, bg_ref, gn_ref, mat_ref, mask_ref,
                o_ref, sout_ref, s_ref, *, c, valid, qscale):
    n = pl.program_id(1)
    heads, dk, dv = s_ref.shape

    @pl.when(n == 0)
    def _():
        s_ref[...] = s0_ref[0]

    mats = mat_ref[...]
    nlev = mask_ref.shape[0] - 1
    hs = range(heads)
    ks = [slice(h * dk, (h + 1) * dk) for h in hs]
    vs = [slice(h * dv, (h + 1) * dv) for h in hs]

    x = jnp.dot(gd_ref[...].astype(BF16), wg_ref[...], preferred_element_type=F32) + bg_ref[...]
    logg = (jnp.minimum(x, 0.0) - jnp.log1p(jnp.exp(-jnp.abs(x)))) * (1.0 / GATE_TAU)
    if valid < c:
        logg = jnp.where(lax.broadcasted_iota(jnp.int32, logg.shape, 0) < valid, logg, 0.0)
    hi = logg.astype(BF16)
    r1 = logg - hi.astype(F32)
    mid = r1.astype(BF16)
    lo = (r1 - mid.astype(F32)).astype(BF16)
    e = jnp.exp(jnp.dot(mats, hi, preferred_element_type=F32)
                + jnp.dot(mats, mid, preferred_element_type=F32)
                + jnp.dot(mats, lo, preferred_element_type=F32))

    q = q_ref[...] * qscale
    k = k_ref[...]
    v = v_ref[...].astype(BF16)
    qb = q.astype(BF16)
    kb = k.astype(BF16)
    q_in = (q * e[0:c]).astype(BF16)
    k_end = (k * e[c:2 * c]).astype(BF16)
    s_old = [s_ref[h] for h in hs]

    o = [jnp.dot(q_in[:, ks[h]], s_old[h].astype(BF16), preferred_element_type=F32) for h in hs]
    a = [mask_ref[nlev] * lax.dot_general(qb[:, ks[h]], kb[:, ks[h]], NT_DIMS, preferred_element_type=F32)
         for h in hs]
    for lv in range(nlev):
        qt = (q * e[(2 + 2 * lv) * c:(3 + 2 * lv) * c]).astype(BF16)
        kt = (k * e[(3 + 2 * lv) * c:(4 + 2 * lv) * c]).astype(BF16)
        a = [a[h] + mask_ref[lv] * lax.dot_general(qt[:, ks[h]], kt[:, ks[h]], NT_DIMS,
                                                   preferred_element_type=F32) for h in hs]
    o = [o[h] + jnp.dot(a[h].astype(BF16), v[:, vs[h]], preferred_element_type=F32) for h in hs]
    upd = [lax.dot_general(k_end[:, ks[h]], v[:, vs[h]], TN_DIMS, preferred_element_type=F32) for h in hs]

    decay_cols = jnp.broadcast_to(e[c - 1:c], (LANES, heads * dk)).T
    for h in hs:
        decay = jnp.concatenate([decay_cols[ks[h]]] * (dv // LANES), axis=1)
        s_ref[h] = decay * s_old[h] + upd[h]
        r = r_ref[:, vs[h]]
        o_ref[:, vs[h]] = (_rms(o[h], gn_ref[...]) * (r * jax.nn.sigmoid(r))).astype(o_ref.dtype)

    @pl.when(n == pl.num_programs(1) - 1)
    def _():
        sout_ref[0] = s_ref[...]


def _gla(z, zcols, gd, s0, batch0, wg, bg, gn, *, nb, seq, c, valid, out_dtype):
    _, heads, dk, dv = s0.shape
    rank = gd.shape[1]
    nc = seq // c
    wk, wv = heads * dk, heads * dv
    qc, kc, vc, rc = zcols
    assert qc % wk == 0 and kc % wk == 0 and vc % wv == 0 and rc % wv == 0
    mats, masks = _gla_tables(c)
    mats = jnp.asarray(mats, BF16)
    masks = jnp.asarray(masks, F32)

    def zspec(w, col0):
        return pl.BlockSpec((c, w), lambda b, n: (b * nc + n, col0 // w))

    return pl.pallas_call(
        functools.partial(_gla_kernel, c=c, valid=valid, qscale=dk ** -0.5),
        grid=(nb, nc),
        in_specs=[
            zspec(wk, qc), zspec(wk, kc), zspec(wv, vc),
            pl.BlockSpec((c, rank), lambda b, n: (b * nc + n, 0)),
            zspec(wv, rc),
            pl.BlockSpec((1, heads, dk, dv), lambda b, n: (batch0 + b, 0, 0, 0)),
            pl.BlockSpec((rank, wk), lambda b, n: (0, 0)),
            pl.BlockSpec((1, wk), lambda b, n: (0, 0)),
            pl.BlockSpec((1, dv), lambda b, n: (0, 0)),
            pl.BlockSpec(mats.shape, lambda b, n: (0, 0)),
            pl.BlockSpec(masks.shape, lambda b, n: (0, 0, 0)),
        ],
        out_specs=[
            pl.BlockSpec((c, wv), lambda b, n: (b * nc + n, 0)),
            pl.BlockSpec((1, heads, dk, dv), lambda b, n: (b, 0, 0, 0)),
        ],
        out_shape=[
            jax.ShapeDtypeStruct((nb * seq, wv), out_dtype),
            jax.ShapeDtypeStruct((nb, heads, dk, dv), F32),
        ],
        scratch_shapes=[pltpu.VMEM((heads, dk, dv), F32)],
        compiler_params=_params("parallel", "arbitrary"),
    )(z, z, z, gd, z, s0, wg, bg, gn, mats, masks)


def _oproj_kernel(oa_ref, og_ref, wa_ref, wg_ref, x_ref, y_ref):
    acc = jnp.dot(oa_ref[...], wa_ref[...], preferred_element_type=F32)
    acc = acc + jnp.dot(og_ref[...], wg_ref[...], preferred_element_type=F32)
    y_ref[...] = x_ref[...] + acc


def _oproj(oa, og, w, x, *, tm=1024, tn=512):
    n, d = x.shape
    wa_rows = oa.shape[1]
    wg_rows = og.shape[1]
    assert wa_rows == wg_rows
    tm = _tile(n, tm)
    tn = _tile(d, tn)
    return pl.pallas_call(
        _oproj_kernel,
        grid=(n // tm, d // tn),
        in_specs=[
            pl.BlockSpec((tm, wa_rows), lambda i, j: (i, 0)),
            pl.BlockSpec((tm, wg_rows), lambda i, j: (i, 0)),
            pl.BlockSpec((wa_rows, tn), lambda i, j: (0, j)),
            pl.BlockSpec((wg_rows, tn), lambda i, j: (1, j)),
            pl.BlockSpec((tm, tn), lambda i, j: (i, j)),
        ],
        out_specs=pl.BlockSpec((tm, tn), lambda i, j: (i, j)),
        out_shape=jax.ShapeDtypeStruct((n, d), F32),
        compiler_params=_params("parallel", "arbitrary"),
    )(oa, og, w, w, x)


def _row(v):
    return v.reshape(1, -1).astype(F32)


def kernel(x_prompt, x_sample, cache_k, cache_v, state_gla, page_table, w_in, attn_q_norm, attn_k_norm, lam_q1, lam_k1, lam_q2, lam_k2, attn_out_norm, gla_w_gate_up, gla_b_gate, gla_out_norm, w_out, ffn1_norm, ffn1_w_gate, ffn1_w_up, ffn1_w_down, mix_norm, ffn2_norm, ffn2_w_gate, ffn2_w_up, ffn2_w_down, final_norm):
    nbp, seq, d = x_prompt.shape
    nbs, ld, _ = x_sample.shape
    depth, n_pool, page, heads_a, w2 = cache_k.shape
    hd = w2 // 2
    vd = cache_v.shape[-1]
    _, _, heads_g, dk, dv = state_gla.shape
    rank = gla_w_gate_up.shape[1]
    past = page_table.shape[1] * page
    rot = hd // 4
    wq = heads_a * w2
    wv = heads_a * vd
    wgq = heads_g * dk
    wgv = heads_g * dv
    c_q, c_k, c_v = 0, wq, 2 * wq
    c_rest = 2 * wq + wv
    w_rest = 2 * wgq + 2 * wgv
    c_gd = c_rest + w_rest
    zcols = (0, wgq, 2 * wgq, 2 * wgq + wgv)

    tab_p = _rope_tables(jnp.arange(seq), hd, rot)
    tab_s = _rope_tables(past + jnp.arange(nbs * ld) % ld, hd, rot)
    chunk = math.gcd(seq, GLA_CHUNK)
    ld_pad = -(-ld // 8) * 8

    xp = x_prompt.reshape(nbp * seq, d)
    xs = x_sample.reshape(nbs * ld, d)
    cache_k4 = cache_k.reshape(depth * n_pool, page, heads_a, w2)
    cache_v4 = cache_v.reshape(depth * n_pool, page, heads_a, vd)
    state4 = state_gla.reshape(depth * nbs, heads_g, dk, dv)
    zero_state = jnp.zeros((nbp, heads_g, dk, dv), F32)
    outs = [[] for _ in range(6)]
    for l in range(depth):
        li = 0.8 - 0.6 * math.exp(-0.3 * l)
        ffn1_w = (ffn1_w_gate[l], ffn1_w_up[l], ffn1_w_down[l])
        ffn2_w = (ffn2_w_gate[l], ffn2_w_up[l], ffn2_w_down[l])
        col0s = (c_q, c_k, c_v, c_rest)
        w_in_parts = [(w_in[l], c) for c in col0s]
        if nbs * ld > FFN_ROWS:
            ffn1_w = tuple(w.astype(BF16) for w in ffn1_w)
            ffn2_w = tuple(w.astype(BF16) for w in ffn2_w)
            w_in_parts = [(w_in[l].astype(BF16), c) for c in col0s]
        w_gd = w_in[l][:, c_gd:].astype(BF16)
        w_out_b = w_out[l].astype(BF16)
        lam_vecs = jnp.stack([lam_q1[l], lam_k1[l], lam_q2[l], lam_k2[l]]).astype(F32)
        gq, gk, gn_a = _row(attn_q_norm[l]), _row(attn_k_norm[l]), _row(attn_out_norm[l])
        wg = gla_w_gate_up[l].astype(BF16)
        bg, gn_g = _row(gla_b_gate[l]), _row(gla_out_norm[l])

        def ffn(x, norm, w, post, final):
            res = _ffn(x, _row(norm), *w, _row(post), final=final)
            return res if w[0].dtype == F32 else (res, w)

        def dense_in(x, tables, q_dtype, w, wparts):
            x1, w = ffn(x, ffn1_norm[l], w, ffn1_norm[l], False)
            h = _norm(x1, _row(mix_norm[l]))
            (wqp, cq), (wkp, ck), (wvp, cv), (wrp, cr) = wparts
            res = [_qk_proj(h, wqp, cq, wq, gq, tables, [q_dtype], rot=rot),
                   _qk_proj(h, wkp, ck, wq, gk, tables, [F32, BF16], rot=rot),
                   _proj(h, wvp, cv, wv, [F32, BF16]),
                   _proj(h, wrp, cr, w_rest, [F32])]
            if wqp.dtype == F32:
                wparts = [(r[-1], 0) for r in res]
                res = [r[:-1] for r in res]
            (qa,), (ka, ka_b), (va, va_b), (z,) = res
            (gd,) = _proj(h, w_gd, 0, rank, [F32])
            return (x1, qa, ka, ka_b, va, va_b, z, gd), w, wparts

        def dense_out(x1, oa, og, w):
            x2 = _oproj(oa, og, w_out_b, x1)
            return ffn(x2, ffn2_norm[l], w, final_norm[l], True)

        (x1, qa, ka, ka_b, va, va_b, z, gd), ffn1_w, w_in_parts = dense_in(xs, tab_s, F32, ffn1_w, w_in_parts)
        oa = _sample_attention(qa.reshape(nbs, ld, wq), ka.reshape(nbs, ld, wq), va.reshape(nbs, ld, wv),
                               cache_k4, cache_v4, page_table, lam_vecs, gn_a, pool0=l * n_pool, hd=hd, li=li)
        pad = lambda a: jnp.pad(a.reshape(nbs, ld, -1), ((0, 0), (0, ld_pad - ld), (0, 0))).reshape(nbs * ld_pad, -1)
        og, s_s = _gla(pad(z), zcols, pad(gd), state4, l * nbs, wg, bg, gn_g,
                       nb=nbs, seq=ld_pad, c=ld_pad, valid=ld, out_dtype=F32)
        og = og.reshape(nbs, ld_pad, wgv)[:, :ld].reshape(nbs * ld, wgv)
        xs, ffn2_w = dense_out(x1, oa.reshape(nbs * ld, wv).astype(BF16), og.astype(BF16), ffn2_w)
        outs[3].append(ka.reshape(nbs, ld, heads_a, w2))
        outs[4].append(va.reshape(nbs, ld, heads_a, vd))
        outs[5].append(s_s)

        (x1, qa, ka, ka_b, va, va_b, z, gd), _, _ = dense_in(xp, tab_p, BF16, ffn1_w, w_in_parts)
        oa = _prompt_attention(qa, ka_b, va_b, lam_vecs, gn_a, nb=nbp, seq=seq, hd=hd, li=li)
        og, s_p = _gla(z, zcols, gd, zero_state, 0, wg, bg, gn_g,
                       nb=nbp, seq=seq, c=chunk, valid=chunk, out_dtype=BF16)
        xp, _ = dense_out(x1, oa, og, ffn2_w)
        outs[0].append(ka.reshape(nbp, seq, heads_a, w2))
        outs[1].append(va.reshape(nbp, seq, heads_a, vd))
        outs[2].append(s_p)

    return (xp.reshape(nbp, seq, d), xs.reshape(nbs, ld, d), jnp.stack(outs[0]), jnp.stack(outs[1]),
            jnp.stack(outs[2]), jnp.stack(outs[3]), jnp.stack(outs[4]), jnp.stack(outs[5]))
```

```python
import functools
import math

import numpy as np
import jax
import jax.numpy as jnp
from jax import lax
from jax.experimental import pallas as pl
from jax.experimental.pallas import tpu as pltpu

F32 = jnp.float32
BF16 = jnp.bfloat16
EPS = 1e-6
ROPE_THETA = 500000.0
GATE_TAU = 16.0
FFN_ROWS = 512
GLA_CHUNK = 128
LANES = 128
MXU_COLS = 256
VMEM_LIMIT = 60 * 1024 * 1024
NT_DIMS = (((1,), (1,)), ((), ()))
TN_DIMS = (((0,), (0,)), ((), ()))


def _params(*sem):
    return pltpu.CompilerParams(dimension_semantics=sem, vmem_limit_bytes=VMEM_LIMIT)


def _rms(x, gain):
    ms = jnp.mean(x * x, axis=-1, keepdims=True)
    return x * lax.rsqrt(ms + EPS) * gain


def _tile(n, want):
    t = min(n, want)
    while n % t:
        t //= 2
    return t


def _ffn_kernel(x_ref, g_ref, wg_ref, wu_ref, wd_ref, pg_ref, *rest, final, emit_bf16, tail):
    rest = list(rest)
    y_ref = rest.pop(0)
    wb_refs = [rest.pop(0) for _ in range(3)] if emit_bf16 else None
    (xn_ref,) = rest
    j = pl.program_id(1)
    last = pl.num_programs(1) - 1
    tf = wd_ref.shape[0]

    @pl.when(j == 0)
    def _():
        xn_ref[...] = _rms(x_ref[...], g_ref[...]).astype(BF16)
        y_ref[...] = jnp.zeros_like(y_ref)

    def accumulate(width):
        if emit_bf16:
            for ref, w_ref in zip(wb_refs, (wg_ref, wu_ref, wd_ref)):
                ref[...] = w_ref[...].astype(BF16)
            wg, wu, wd = (ref[...] for ref in wb_refs)
        else:
            wg, wu, wd = wg_ref[:, :width], wu_ref[:, :width], wd_ref[:width, :]
        xn = xn_ref[...]
        g = jnp.dot(xn, wg, preferred_element_type=F32)
        u = jnp.dot(xn, wu, preferred_element_type=F32)
        h = (g * jax.nn.sigmoid(g) * u).astype(BF16)
        y_ref[...] += jnp.dot(h, wd, preferred_element_type=F32)

    if tail == tf:
        accumulate(tf)
    else:
        pl.when(j < last)(functools.partial(accumulate, tf))
        pl.when(j == last)(functools.partial(accumulate, tail))

    @pl.when(j == last)
    def _():
        y = x_ref[...] + 0.5 * y_ref[...]
        y_ref[...] = _rms(y, pg_ref[...]) if final else y


def _ffn(x, norm_g, wg, wu, wd, post_g, *, final, tm=FFN_ROWS, tf=512):
    n, d = x.shape
    dff = wd.shape[0]
    emit_bf16 = wd.dtype == F32
    y_mode = {}
    if emit_bf16:
        tf = _tile(dff, tf // 4)
        assert n <= tm
        y_mode = dict(pipeline_mode=pl.Buffered(1))
    nf = pl.cdiv(dff, tf)
    tail = dff - (nf - 1) * tf
    tm = _tile(n, tm)
    out_shape = [jax.ShapeDtypeStruct((n, d), F32)]
    out_specs = [pl.BlockSpec((tm, d), lambda i, j: (i, 0), **y_mode)]
    if emit_bf16:
        out_shape += [jax.ShapeDtypeStruct(w.shape, BF16) for w in (wg, wu, wd)]
        out_specs += [pl.BlockSpec((d, tf), lambda i, j: (0, j)), pl.BlockSpec((d, tf), lambda i, j: (0, j)),
                      pl.BlockSpec((tf, d), lambda i, j: (j, 0))]
    res = pl.pallas_call(
        functools.partial(_ffn_kernel, final=final, emit_bf16=emit_bf16, tail=tail),
        grid=(n // tm, nf),
        in_specs=[
            pl.BlockSpec((tm, d), lambda i, j: (i, 0), pipeline_mode=pl.Buffered(1)),
            pl.BlockSpec((1, d), lambda i, j: (0, 0)),
            pl.BlockSpec((d, tf), lambda i, j: (0, j)),
            pl.BlockSpec((d, tf), lambda i, j: (0, j)),
            pl.BlockSpec((tf, d), lambda i, j: (j, 0)),
            pl.BlockSpec((1, d), lambda i, j: (0, 0)),
        ],
        out_specs=out_specs,
        out_shape=out_shape,
        scratch_shapes=[pltpu.VMEM((tm, d), BF16)],
        compiler_params=_params("parallel", "arbitrary"),
    )(x, norm_g, wg, wu, wd, post_g)
    return (res[0], tuple(res[1:])) if emit_bf16 else res[0]


def _norm_kernel(x_ref, g_ref, o_ref):
    o_ref[...] = _rms(x_ref[...], g_ref[...]).astype(o_ref.dtype)


def _norm(x, gain, *, tm=FFN_ROWS):
    n, d = x.shape
    tm = _tile(n, tm)
    return pl.pallas_call(
        _norm_kernel,
        grid=(n // tm,),
        in_specs=[pl.BlockSpec((tm, d), lambda i: (i, 0)), pl.BlockSpec((1, d), lambda i: (0, 0))],
        out_specs=pl.BlockSpec((tm, d), lambda i: (i, 0)),
        out_shape=jax.ShapeDtypeStruct((n, d), BF16),
        compiler_params=_params("parallel"),
    )(x, gain)


def _proj_kernel(h_ref, w_ref, *outs):
    acc = jnp.dot(h_ref[...], w_ref[...], preferred_element_type=F32)
    for o in outs:
        o[...] = acc.astype(o.dtype)


def _proj(h, w, col0, ncols, dtypes, *, tm=1024, tn=512):
    n, d = h.shape
    tm = _tile(n, tm)
    tn = _tile(ncols, tn)
    if ncols == w.shape[1]:
        off = 0
    else:
        assert col0 % tn == 0
        off = col0 // tn
    return pl.pallas_call(
        _proj_kernel,
        grid=(n // tm, ncols // tn),
        in_specs=[
            pl.BlockSpec((tm, d), lambda i, j: (i, 0)),
            pl.BlockSpec((d, tn), lambda i, j: (0, off + j)),
        ],
        out_specs=[pl.BlockSpec((tm, tn), lambda i, j: (i, j)) for _ in dtypes],
        out_shape=[jax.ShapeDtypeStruct((n, ncols), dt) for dt in dtypes],
        compiler_params=_params("parallel", "arbitrary"),
    )(h, w)


def _qk_kernel(h_ref, w_ref, gain_ref, c_ref, s_ref, perm_ref, *outs, tn, sub):
    h = h_ref[...]
    gain = gain_ref[...]
    reps = sub // LANES
    cos = jnp.concatenate([c_ref[...]] * reps, axis=1)
    sin = jnp.concatenate([s_ref[...]] * reps, axis=1)
    perm = perm_ref[...]
    acc = jnp.dot(h, w_ref[...], preferred_element_type=F32)
    for s0 in range(0, tn, sub):
        xn = jnp.concatenate([_rms(acc[:, c0:c0 + LANES], gain) for c0 in range(s0, s0 + sub, LANES)], axis=1)
        hi = xn.astype(BF16)
        lo = (xn - hi.astype(F32)).astype(BF16)
        partner = jnp.dot(jnp.concatenate([hi, lo], axis=1), perm, preferred_element_type=F32)
        y = xn * cos + partner * sin
        for o in outs:
            o[:, s0:s0 + sub] = y.astype(o.dtype)


def _rope_perm(hd, rot, sub):
    half = rot // 2
    p = np.zeros((hd, hd), np.float32)
    for l in range(half):
        p[l + half, l] = -1.0
        p[l, l + half] = 1.0
    bd = np.kron(np.eye(sub // hd, dtype=np.float32), p)
    return np.concatenate([bd, bd], axis=0)


def _qk_proj(h, w, col0, ncols, gain, tables, dtypes, *, rot, tm=1024, tn=512):
    n, d = h.shape
    tm = _tile(n, tm)
    tn = _tile(ncols, tn)
    cos, sin = tables
    hd = cos.shape[1]
    tm = _tile(cos.shape[0], tm)
    nt = cos.shape[0] // tm
    off = col0 // tn
    sub = min(tn, MXU_COLS)
    perm = jnp.asarray(_rope_perm(hd, rot, sub), BF16)
    tspec = pl.BlockSpec((tm, hd), lambda i, j: (i % nt, 0))
    return pl.pallas_call(
        functools.partial(_qk_kernel, tn=tn, sub=sub),
        grid=(n // tm, ncols // tn),
        in_specs=[
            pl.BlockSpec((tm, d), lambda i, j: (i, 0)),
            pl.BlockSpec((d, tn), lambda i, j: (0, off + j)),
            pl.BlockSpec((1, hd), lambda i, j: (0, 0)),
            tspec, tspec,
            pl.BlockSpec(perm.shape, lambda i, j: (0, 0)),
        ],
        out_specs=[pl.BlockSpec((tm, tn), lambda i, j: (i, j)) for _ in dtypes],
        out_shape=[jax.ShapeDtypeStruct((n, ncols), dt) for dt in dtypes],
        compiler_params=_params("parallel", "arbitrary"),
    )(h, w, gain, cos, sin, perm)


def _rope_tables(pos, hd, rot):
    half = rot // 2
    inv = ROPE_THETA ** (-jnp.arange(half, dtype=F32) * 2.0 / rot)
    ang = pos.astype(F32)[:, None] * inv[None, :]
    cos, sin = jnp.cos(ang), jnp.sin(ang)
    n = pos.shape[0]
    c = jnp.concatenate([cos, cos, jnp.ones((n, hd - rot), F32)], axis=1)
    s = jnp.concatenate([sin, sin, jnp.zeros((n, hd - rot), F32)], axis=1)
    return c, s


def _lambda(lam_ref, li):
    lv = lam_ref[...]
    a = jnp.sum(lv[0:1] * lv[1:2], axis=-1, keepdims=True)
    b = jnp.sum(lv[2:3] * lv[3:4], axis=-1, keepdims=True)
    return jnp.exp(a) - jnp.exp(b) + li


def _pattn_kernel(q_ref, k_ref, v_ref, lam_ref, gn_ref, o_ref, *, t, nq, hd, li):
    c1 = hd ** -0.5 * math.log2(math.e)
    lam = _lambda(lam_ref, li)
    row = lax.broadcasted_iota(jnp.int32, (t, t), 0)
    col = lax.broadcasted_iota(jnp.int32, (t, t), 1)

    for n in range(nq):
        past = n * t
        q = q_ref[past:past + t, :]
        k = k_ref[0:past + t, :]
        e, inv = [], []
        for c in range(2):
            s = lax.dot_general(q[:, c * hd:(c + 1) * hd], k[:, c * hd:(c + 1) * hd], NT_DIMS,
                                preferred_element_type=F32)
            diag = jnp.where(row >= col, s[:, past:], -jnp.inf)
            s = jnp.concatenate([s[:, :past], diag], axis=1) if past else diag
            ec = jnp.exp2(s * c1 - jnp.max(s, axis=-1, keepdims=True) * c1)
            e.append(ec)
            inv.append(1.0 / jnp.sum(ec, axis=-1, keepdims=True))
        w = (e[0] * inv[0] - e[1] * (lam * inv[1])).astype(BF16)
        o = jnp.dot(w, v_ref[0:past + t, :], preferred_element_type=F32)
        o_ref[past:past + t, :] = (_rms(o, gn_ref[...]) * (1.0 - li)).astype(o_ref.dtype)


def _prompt_attention(q, k, v, lam_vecs, gn, *, nb, seq, hd, li, t=256):
    n, width = q.shape
    heads = width // (2 * hd)
    vd = v.shape[1] // heads
    t = _tile(seq, t)
    return pl.pallas_call(
        functools.partial(_pattn_kernel, t=t, nq=seq // t, hd=hd, li=li),
        grid=(nb, heads),
        in_specs=[
            pl.BlockSpec((seq, 2 * hd), lambda b, h: (b, h)),
            pl.BlockSpec((seq, 2 * hd), lambda b, h: (b, h)),
            pl.BlockSpec((seq, vd), lambda b, h: (b, h)),
            pl.BlockSpec((4, hd), lambda b, h: (0, 0)),
            pl.BlockSpec((1, vd), lambda b, h: (0, 0)),
        ],
        out_specs=pl.BlockSpec((seq, vd), lambda b, h: (b, h)),
        out_shape=jax.ShapeDtypeStruct((n, heads * vd), BF16),
        compiler_params=_params("parallel", "parallel"),
    )(q, k, v, lam_vecs, gn)


def _sattn_kernel(pt_ref, q_ref, kn_ref, vn_ref, lam_ref, gn_ref, *rest, G, heads, hd, ld, li):
    kp = rest[:G]
    vp = rest[G:2 * G]
    o_ref, qm_ref, m_ref, l_ref, acc_ref = rest[2 * G:]
    p = pl.program_id(1)
    scale = hd ** -0.5
    w2 = 2 * hd
    vd = acc_ref.shape[1]
    rows = 2 * ld
    page = kp[0].shape[1]
    pcols = page * heads

    @pl.when(p == 0)
    def _():
        row = lax.broadcasted_iota(jnp.int32, (rows, w2), 0)
        lane = lax.broadcasted_iota(jnp.int32, (rows, w2), 1)
        own_map = (row < ld) == (lane < hd)
        qi = lax.broadcasted_iota(jnp.int32, (rows, 1), 0) % ld
        qm, m0, l0, a0 = [], [], [], []
        for h in range(heads):
            qh = q_ref[0][:, h * w2:(h + 1) * w2]
            qf = jnp.where(own_map, jnp.concatenate([qh, qh], axis=0), 0.0)
            kn = kn_ref[0][:, h * w2:(h + 1) * w2]
            vn = vn_ref[0][:, h * vd:(h + 1) * vd]
            s = [jnp.where(qi >= j, jnp.sum(qf * kn[j:j + 1, :], axis=-1, keepdims=True) * scale, -jnp.inf)
                 for j in range(ld)]
            m = functools.reduce(jnp.maximum, s)
            pj = [jnp.exp(sj - m) for sj in s]
            qm.append(qf)
            m0.append(m)
            l0.append(functools.reduce(jnp.add, pj))
            a0.append(functools.reduce(jnp.add, [pj[j] * vn[j:j + 1, :] for j in range(ld)]))
        qm_ref[...] = jnp.concatenate(qm, axis=0).astype(BF16)
        m_ref[...] = jnp.concatenate(m0, axis=0)
        l_ref[...] = jnp.concatenate(l0, axis=0)
        acc_ref[...] = jnp.concatenate(a0, axis=0)

    nr = heads * rows
    qmat = qm_ref[...]
    same_head = (lax.broadcasted_iota(jnp.int32, (nr, pcols), 0) // rows
                 == lax.broadcasted_iota(jnp.int32, (nr, pcols), 1) % heads)
    s = jnp.concatenate(
        [jnp.where(same_head,
                   lax.dot_general(qmat, kp[g][0].reshape(pcols, w2).astype(BF16), NT_DIMS,
                                   preferred_element_type=F32) * scale, -jnp.inf)
         for g in range(G)], axis=1)
    m_old = m_ref[...]
    m_new = jnp.maximum(m_old, jnp.max(s, axis=-1, keepdims=True))
    alpha = jnp.exp(m_old - m_new)
    pr = jnp.exp(s - m_new)
    pv = functools.reduce(jnp.add, [
        jnp.dot(pr[:, g * pcols:(g + 1) * pcols].astype(BF16), vp[g][0].reshape(pcols, vd).astype(BF16),
                preferred_element_type=F32) for g in range(G)])
    m_ref[...] = m_new
    l_ref[...] = alpha * l_ref[...] + jnp.sum(pr, axis=-1, keepdims=True)
    acc_ref[...] = alpha * acc_ref[...] + pv

    @pl.when(p == pl.num_programs(1) - 1)
    def _():
        lam = _lambda(lam_ref, li)
        w = acc_ref[...] / l_ref[...]
        for h in range(heads):
            o = w[h * rows:h * rows + ld] - lam * w[h * rows + ld:(h + 1) * rows]
            o_ref[0, :, h * vd:(h + 1) * vd] = _rms(o, gn_ref[...]) * (1.0 - li)


def _sample_attention(q, kn, vn, cache_k, cache_v, page_table, lam_vecs, gn, *, pool0, hd, li, G=8):
    nb, ld, width = q.shape
    _, page, heads, w2 = cache_k.shape
    vwidth = vn.shape[2]
    vd = vwidth // heads
    n_pages = page_table.shape[1]
    G = _tile(n_pages, G)

    def page_spec(w, g):
        return pl.BlockSpec((1, page, heads, w), lambda b, p, pt: (pool0 + pt[b, p * G + g], 0, 0, 0))

    grid_spec = pltpu.PrefetchScalarGridSpec(
        num_scalar_prefetch=1,
        grid=(nb, n_pages // G),
        in_specs=[
            pl.BlockSpec((1, ld, width), lambda b, p, pt: (b, 0, 0)),
            pl.BlockSpec((1, ld, width), lambda b, p, pt: (b, 0, 0)),
            pl.BlockSpec((1, ld, vwidth), lambda b, p, pt: (b, 0, 0)),
            pl.BlockSpec((4, hd), lambda b, p, pt: (0, 0)),
            pl.BlockSpec((1, vd), lambda b, p, pt: (0, 0)),
        ] + [page_spec(w2, g) for g in range(G)] + [page_spec(vd, g) for g in range(G)],
        out_specs=pl.BlockSpec((1, ld, vwidth), lambda b, p, pt: (b, 0, 0)),
        scratch_shapes=[
            pltpu.VMEM((heads * 2 * ld, w2), BF16),
            pltpu.VMEM((heads * 2 * ld, 1), F32),
            pltpu.VMEM((heads * 2 * ld, 1), F32),
            pltpu.VMEM((heads * 2 * ld, vd), F32),
        ],
    )
    return pl.pallas_call(
        functools.partial(_sattn_kernel, G=G, heads=heads, hd=hd, ld=ld, li=li),
        grid_spec=grid_spec,
        out_shape=jax.ShapeDtypeStruct((nb, ld, vwidth), F32),
        compiler_params=_params("parallel", "arbitrary"),
    )(page_table, q, kn, vn, lam_vecs, gn, *([cache_k] * G), *([cache_v] * G))


def _gla_tables(c):
    t = np.arange(c)[:, None]
    u = np.arange(c)[None, :]
    mats = [u <= t, u > t]
    masks = []
    m = c // 2
    while m >= 1:
        mats.append((u > (t // m) * m) & (u <= t))
        mats.append((u > t) & (u <= (t // m + 1) * m))
        masks.append(((t // m) % 2 == 1) & ((u // m) == (t // m) - 1))
        m //= 2
    masks.append(t == u)
    return (np.concatenate(mats, 0).astype(np.float32), np.stack(masks, 0).astype(np.float32))


def _gla_kernel(q_ref, k_ref, v_ref, gd_ref, r_ref, s0_ref, wg_ref, bg_ref, gn_ref, mat_ref, mask_ref,
                o_ref, sout_ref, s_ref, *, c, valid, qscale):
    n = pl.program_id(1)
    heads, dk, dv = s_ref.shape

    @pl.when(n == 0)
    def _():
        s_ref[...] = s0_ref[0]

    mats = mat_ref[...]
    nlev = mask_ref.shape[0] - 1
    hs = range(heads)
    ks = [slice(h * dk, (h + 1) * dk) for h in hs]
    vs = [slice(h * dv, (h + 1) * dv) for h in hs]

    x = jnp.dot(gd_ref[...].astype(BF16), wg_ref[...], preferred_element_type=F32) + bg_ref[...]
    logg = (jnp.minimum(x, 0.0) - jnp.log1p(jnp.exp(-jnp.abs(x)))) * (1.0 / GATE_TAU)
    if valid < c:
        logg = jnp.where(lax.broadcasted_iota(jnp.int32, logg.shape, 0) < valid, logg, 0.0)
    hi = logg.astype(BF16)
    lo = (logg - hi.astype(F32)).astype(BF16)
    e = jnp.exp(jnp.dot(mats, hi, preferred_element_type=F32)
                + jnp.dot(mats, lo, preferred_element_type=F32))

    q = q_ref[...] * qscale
    k = k_ref[...]
    v = v_ref[...].astype(BF16)
    qb = q.astype(BF16)
    kb = k.astype(BF16)
    q_in = (q * e[0:c]).astype(BF16)
    k_end = (k * e[c:2 * c]).astype(BF16)
    s_old = [s_ref[h] for h in hs]

    o = [jnp.dot(q_in[:, ks[h]], s_old[h].astype(BF16), preferred_element_type=F32) for h in hs]
    a = [mask_ref[nlev] * lax.dot_general(qb[:, ks[h]], kb[:, ks[h]], NT_DIMS, preferred_element_type=F32)
         for h in hs]
    for lv in range(nlev):
        qt = (q * e[(2 + 2 * lv) * c:(3 + 2 * lv) * c]).astype(BF16)
        kt = (k * e[(3 + 2 * lv) * c:(4 + 2 * lv) * c]).astype(BF16)
        a = [a[h] + mask_ref[lv] * lax.dot_general(qt[:, ks[h]], kt[:, ks[h]], NT_DIMS,
                                                   preferred_element_type=F32) for h in hs]
    o = [o[h] + jnp.dot(a[h].astype(BF16), v[:, vs[h]], preferred_element_type=F32) for h in hs]
    upd = [lax.dot_general(k_end[:, ks[h]], v[:, vs[h]], TN_DIMS, preferred_element_type=F32) for h in hs]

    decay_cols = jnp.broadcast_to(e[c - 1:c], (LANES, heads * dk)).T
    for h in hs:
        decay = jnp.concatenate([decay_cols[ks[h]]] * (dv // LANES), axis=1)
        s_ref[h] = decay * s_old[h] + upd[h]
        r = r_ref[:, vs[h]]
        o_ref[:, vs[h]] = (_rms(o[h], gn_ref[...]) * (r * jax.nn.sigmoid(r))).astype(o_ref.dtype)

    @pl.when(n == pl.num_programs(1) - 1)
    def _():
        sout_ref[0] = s_ref[...]


def _gla(z, zcols, gd, s0, batch0, wg, bg, gn, *, nb, seq, c, valid, out_dtype):
    _, heads, dk, dv = s0.shape
    rank = gd.shape[1]
    nc = seq // c
    wk, wv = heads * dk, heads * dv
    qc, kc, vc, rc = zcols
    assert qc % wk == 0 and kc % wk == 0 and vc % wv == 0 and rc % wv == 0
    mats, masks = _gla_tables(c)
    mats = jnp.asarray(mats, BF16)
    masks = jnp.asarray(masks, F32)

    def zspec(w, col0):
        return pl.BlockSpec((c, w), lambda b, n: (b * nc + n, col0 // w))

    return pl.pallas_call(
        functools.partial(_gla_kernel, c=c, valid=valid, qscale=dk ** -0.5),
        grid=(nb, nc),
        in_specs=[
            zspec(wk, qc), zspec(wk, kc), zspec(wv, vc),
            pl.BlockSpec((c, rank), lambda b, n: (b * nc + n, 0)),
            zspec(wv, rc),
            pl.BlockSpec((1, heads, dk, dv), lambda b, n: (batch0 + b, 0, 0, 0)),
            pl.BlockSpec((rank, wk), lambda b, n: (0, 0)),
            pl.BlockSpec((1, wk), lambda b, n: (0, 0)),
            pl.BlockSpec((1, dv), lambda b, n: (0, 0)),
            pl.BlockSpec(mats.shape, lambda b, n: (0, 0)),
            pl.BlockSpec(masks.shape, lambda b, n: (0, 0, 0)),
        ],
        out_specs=[
            pl.BlockSpec((c, wv), lambda b, n: (b * nc + n, 0)),
            pl.BlockSpec((1, heads, dk, dv), lambda b, n: (b, 0, 0, 0)),
        ],
        out_shape=[
            jax.ShapeDtypeStruct((nb * seq, wv), out_dtype),
            jax.ShapeDtypeStruct((nb, heads, dk, dv), F32),
        ],
        scratch_shapes=[pltpu.VMEM((heads, dk, dv), F32)],
        compiler_params=_params("parallel", "arbitrary"),
    )(z, z, z, gd, z, s0, wg, bg, gn, mats, masks)


def _oproj_kernel(oa_ref, og_ref, wa_ref, wg_ref, x_ref, y_ref):
    acc = jnp.dot(oa_ref[...], wa_ref[...], preferred_element_type=F32)
    acc = acc + jnp.dot(og_ref[...], wg_ref[...], preferred_element_type=F32)
    y_ref[...] = x_ref[...] + acc


def _oproj(oa, og, w, x, *, tm=1024, tn=512):
    n, d = x.shape
    wa_rows = oa.shape[1]
    wg_rows = og.shape[1]
    assert wa_rows == wg_rows
    tm = _tile(n, tm)
    tn = _tile(d, tn)
    return pl.pallas_call(
        _oproj_kernel,
        grid=(n // tm, d // tn),
        in_specs=[
            pl.BlockSpec((tm, wa_rows), lambda i, j: (i, 0)),
            pl.BlockSpec((tm, wg_rows), lambda i, j: (i, 0)),
            pl.BlockSpec((wa_rows, tn), lambda i, j: (0, j)),
            pl.BlockSpec((wg_rows, tn), lambda i, j: (1, j)),
            pl.BlockSpec((tm, tn), lambda i, j: (i, j)),
        ],
        out_specs=pl.BlockSpec((tm, tn), lambda i, j: (i, j)),
        out_shape=jax.ShapeDtypeStruct((n, d), F32),
        compiler_params=_params("parallel", "arbitrary"),
    )(oa, og, w, w, x)


def _row(v):
    return v.reshape(1, -1).astype(F32)


def kernel(x_prompt, x_sample, cache_k, cache_v, state_gla, page_table, w_in, attn_q_norm, attn_k_norm, lam_q1, lam_k1, lam_q2, lam_k2, attn_out_norm, gla_w_gate_up, gla_b_gate, gla_out_norm, w_out, ffn1_norm, ffn1_w_gate, ffn1_w_up, ffn1_w_down, mix_norm, ffn2_norm, ffn2_w_gate, ffn2_w_up, ffn2_w_down, final_norm):
    nbp, seq, d = x_prompt.shape
    nbs, ld, _ = x_sample.shape
    depth, n_pool, page, heads_a, w2 = cache_k.shape
    hd = w2 // 2
    vd = cache_v.shape[-1]
    _, _, heads_g, dk, dv = state_gla.shape
    rank = gla_w_gate_up.shape[1]
    past = page_table.shape[1] * page
    rot = hd // 4
    wq = heads_a * w2
    wv = heads_a * vd
    wgq = heads_g * dk
    wgv = heads_g * dv
    c_q, c_k, c_v = 0, wq, 2 * wq
    c_rest = 2 * wq + wv
    w_rest = 2 * wgq + 2 * wgv
    c_gd = c_rest + w_rest
    zcols = (0, wgq, 2 * wgq, 2 * wgq + wgv)

    tab_p = _rope_tables(jnp.arange(seq), hd, rot)
    tab_s = _rope_tables(past + jnp.arange(nbs * ld) % ld, hd, rot)
    chunk = math.gcd(seq, GLA_CHUNK)
    ld_pad = -(-ld // 8) * 8

    xp = x_prompt.reshape(nbp * seq, d)
    xs = x_sample.reshape(nbs * ld, d)
    cache_k4 = cache_k.reshape(depth * n_pool, page, heads_a, w2)
    cache_v4 = cache_v.reshape(depth * n_pool, page, heads_a, vd)
    state4 = state_gla.reshape(depth * nbs, heads_g, dk, dv)
    zero_state = jnp.zeros((nbp, heads_g, dk, dv), F32)
    outs = [[] for _ in range(6)]
    for l in range(depth):
        li = 0.8 - 0.6 * math.exp(-0.3 * l)
        ffn1_w = (ffn1_w_gate[l], ffn1_w_up[l], ffn1_w_down[l])
        ffn2_w = (ffn2_w_gate[l], ffn2_w_up[l], ffn2_w_down[l])
        if nbs * ld > FFN_ROWS:
            ffn1_w = tuple(w.astype(BF16) for w in ffn1_w)
            ffn2_w = tuple(w.astype(BF16) for w in ffn2_w)
        w_in_b = w_in[l][:, :c_gd].astype(BF16)
        w_gd = w_in[l][:, c_gd:].astype(BF16)
        w_out_b = w_out[l].astype(BF16)
        lam_vecs = jnp.stack([lam_q1[l], lam_k1[l], lam_q2[l], lam_k2[l]]).astype(F32)
        gq, gk, gn_a = _row(attn_q_norm[l]), _row(attn_k_norm[l]), _row(attn_out_norm[l])
        wg = gla_w_gate_up[l].astype(BF16)
        bg, gn_g = _row(gla_b_gate[l]), _row(gla_out_norm[l])

        def ffn(x, norm, w, post, final):
            res = _ffn(x, _row(norm), *w, _row(post), final=final)
            return res if w[0].dtype == F32 else (res, w)

        def dense_in(x, tables, q_dtype, w):
            x1, w = ffn(x, ffn1_norm[l], w, ffn1_norm[l], False)
            h = _norm(x1, _row(mix_norm[l]))
            (qa,) = _qk_proj(h, w_in_b, c_q, wq, gq, tables, [q_dtype], rot=rot)
            ka, ka_b = _qk_proj(h, w_in_b, c_k, wq, gk, tables, [F32, BF16], rot=rot)
            va, va_b = _proj(h, w_in_b, c_v, wv, [F32, BF16])
            (z,) = _proj(h, w_in_b, c_rest, w_rest, [F32])
            (gd,) = _proj(h, w_gd, 0, rank, [F32])
            return (x1, qa, ka, ka_b, va, va_b, z, gd), w

        def dense_out(x1, oa, og, w):
            x2 = _oproj(oa, og, w_out_b, x1)
            return ffn(x2, ffn2_norm[l], w, final_norm[l], True)

        (x1, qa, ka, ka_b, va, va_b, z, gd), ffn1_w = dense_in(xs, tab_s, F32, ffn1_w)
        oa = _sample_attention(qa.reshape(nbs, ld, wq), ka.reshape(nbs, ld, wq), va.reshape(nbs, ld, wv),
                               cache_k4, cache_v4, page_table, lam_vecs, gn_a, pool0=l * n_pool, hd=hd, li=li)
        pad = lambda a: jnp.pad(a.reshape(nbs, ld, -1), ((0, 0), (0, ld_pad - ld), (0, 0))).reshape(nbs * ld_pad, -1)
        og, s_s = _gla(pad(z), zcols, pad(gd), state4, l * nbs, wg, bg, gn_g,
                       nb=nbs, seq=ld_pad, c=ld_pad, valid=ld, out_dtype=F32)
        og = og.reshape(nbs, ld_pad, wgv)[:, :ld].reshape(nbs * ld, wgv)
        xs, ffn2_w = dense_out(x1, oa.reshape(nbs * ld, wv).astype(BF16), og.astype(BF16), ffn2_w)
        outs[3].append(ka.reshape(nbs, ld, heads_a, w2))
        outs[4].append(va.reshape(nbs, ld, heads_a, vd))
        outs[5].append(s_s)

        (x1, qa, ka, ka_b, va, va_b, z, gd), _ = dense_in(xp, tab_p, BF16, ffn1_w)
        oa = _prompt_attention(qa, ka_b, va_b, lam_vecs, gn_a, nb=nbp, seq=seq, hd=hd, li=li)
        og, s_p = _gla(z, zcols, gd, zero_state, 0, wg, bg, gn_g,
                       nb=nbp, seq=seq, c=chunk, valid=chunk, out_dtype=BF16)
        xp, _ = dense_out(x1, oa, og, ffn2_w)
        outs[0].append(ka.reshape(nbp, seq, heads_a, w2))
        outs[1].append(va.reshape(nbp, seq, heads_a, vd))
        outs[2].append(s_p)

    return (xp.reshape(nbp, seq, d), xs.reshape(nbs, ld, d), jnp.stack(outs[0]), jnp.stack(outs[1]),
            jnp.stack(outs[2]), jnp.stack(outs[3]), jnp.stack(outs[4]), jnp.stack(outs[5]))
```

```python
import functools
import math

import numpy as np
import jax
import jax.numpy as jnp
from jax import lax
from jax.experimental import pallas as pl
from jax.experimental.pallas import tpu as pltpu

F32 = jnp.float32
BF16 = jnp.bfloat16
EPS = 1e-6
ROPE_THETA = 500000.0
GATE_TAU = 16.0
FFN_ROWS = 512
GLA_CHUNK = 128
LANES = 128
MXU_COLS = 256
VMEM_LIMIT = 60 * 1024 * 1024
NT_DIMS = (((1,), (1,)), ((), ()))
TN_DIMS = (((0,), (0,)), ((), ()))


def _params(*sem):
    return pltpu.CompilerParams(dimension_semantics=sem, vmem_limit_bytes=VMEM_LIMIT)


def _rms(x, gain):
    ms = jnp.mean(x * x, axis=-1, keepdims=True)
    return x * lax.rsqrt(ms + EPS) * gain


def _tile(n, want):
    t = min(n, want)
    while n % t:
        t //= 2
    return t


def _ffn_kernel(x_ref, g_ref, wg_ref, wu_ref, wd_ref, pg_ref, *rest, final, emit_bf16, tail):
    rest = list(rest)
    y_ref = rest.pop(0)
    wb_refs = [rest.pop(0) for _ in range(3)] if emit_bf16 else None
    (xn_ref,) = rest
    j = pl.program_id(1)
    last = pl.num_programs(1) - 1
    tf = wd_ref.shape[0]

    @pl.when(j == 0)
    def _():
        xn_ref[...] = _rms(x_ref[...], g_ref[...]).astype(BF16)
        y_ref[...] = jnp.zeros_like(y_ref)

    def accumulate(width):
        if emit_bf16:
            for ref, w_ref in zip(wb_refs, (wg_ref, wu_ref, wd_ref)):
                ref[...] = w_ref[...].astype(BF16)
            wg, wu, wd = (ref[...] for ref in wb_refs)
        else:
            wg, wu, wd = wg_ref[:, :width], wu_ref[:, :width], wd_ref[:width, :]
        xn = xn_ref[...]
        g = jnp.dot(xn, wg, preferred_element_type=F32)
        u = jnp.dot(xn, wu, preferred_element_type=F32)
        h = (g * jax.nn.sigmoid(g) * u).astype(BF16)
        y_ref[...] += jnp.dot(h, wd, preferred_element_type=F32)

    if tail == tf:
        accumulate(tf)
    else:
        pl.when(j < last)(functools.partial(accumulate, tf))
        pl.when(j == last)(functools.partial(accumulate, tail))

    @pl.when(j == last)
    def _():
        y = x_ref[...] + 0.5 * y_ref[...]
        y_ref[...] = _rms(y, pg_ref[...]) if final else y


def _ffn(x, norm_g, wg, wu, wd, post_g, *, final, emit_bf16, tm=FFN_ROWS, tf=512):
    n, d = x.shape
    dff = wd.shape[0]
    y_mode = {}
    if emit_bf16:
        tf = _tile(dff, tf // 4)
        assert n <= tm
        y_mode = dict(pipeline_mode=pl.Buffered(1))
    nf = pl.cdiv(dff, tf)
    tail = dff - (nf - 1) * tf
    tm = _tile(n, tm)
    out_shape = [jax.ShapeDtypeStruct((n, d), F32)]
    out_specs = [pl.BlockSpec((tm, d), lambda i, j: (i, 0), **y_mode)]
    if emit_bf16:
        out_shape += [jax.ShapeDtypeStruct(w.shape, BF16) for w in (wg, wu, wd)]
        out_specs += [pl.BlockSpec((d, tf), lambda i, j: (0, j)), pl.BlockSpec((d, tf), lambda i, j: (0, j)),
                      pl.BlockSpec((tf, d), lambda i, j: (j, 0))]
    res = pl.pallas_call(
        functools.partial(_ffn_kernel, final=final, emit_bf16=emit_bf16, tail=tail),
        grid=(n // tm, nf),
        in_specs=[
            pl.BlockSpec((tm, d), lambda i, j: (i, 0), pipeline_mode=pl.Buffered(1)),
            pl.BlockSpec((1, d), lambda i, j: (0, 0)),
            pl.BlockSpec((d, tf), lambda i, j: (0, j)),
            pl.BlockSpec((d, tf), lambda i, j: (0, j)),
            pl.BlockSpec((tf, d), lambda i, j: (j, 0)),
            pl.BlockSpec((1, d), lambda i, j: (0, 0)),
        ],
        out_specs=out_specs,
        out_shape=out_shape,
        scratch_shapes=[pltpu.VMEM((tm, d), BF16)],
        compiler_params=_params("parallel", "arbitrary"),
    )(x, norm_g, wg, wu, wd, post_g)
    return (res[0], tuple(res[1:])) if emit_bf16 else res[0]


def _norm_kernel(x_ref, g_ref, o_ref):
    o_ref[...] = _rms(x_ref[...], g_ref[...]).astype(o_ref.dtype)


def _norm(x, gain, *, tm=FFN_ROWS):
    n, d = x.shape
    tm = _tile(n, tm)
    return pl.pallas_call(
        _norm_kernel,
        grid=(n // tm,),
        in_specs=[pl.BlockSpec((tm, d), lambda i: (i, 0)), pl.BlockSpec((1, d), lambda i: (0, 0))],
        out_specs=pl.BlockSpec((tm, d), lambda i: (i, 0)),
        out_shape=jax.ShapeDtypeStruct((n, d), BF16),
        compiler_params=_params("parallel"),
    )(x, gain)


def _proj_kernel(h_ref, w_ref, *outs):
    acc = jnp.dot(h_ref[...], w_ref[...], preferred_element_type=F32)
    for o in outs:
        o[...] = acc.astype(o.dtype)


def _proj(h, w, col0, ncols, dtypes, *, tm=1024, tn=512):
    n, d = h.shape
    tm = _tile(n, tm)
    tn = _tile(ncols, tn)
    if ncols == w.shape[1]:
        off = 0
    else:
        assert col0 % tn == 0
        off = col0 // tn
    return pl.pallas_call(
        _proj_kernel,
        grid=(n // tm, ncols // tn),
        in_specs=[
            pl.BlockSpec((tm, d), lambda i, j: (i, 0)),
            pl.BlockSpec((d, tn), lambda i, j: (0, off + j)),
        ],
        out_specs=[pl.BlockSpec((tm, tn), lambda i, j: (i, j)) for _ in dtypes],
        out_shape=[jax.ShapeDtypeStruct((n, ncols), dt) for dt in dtypes],
        compiler_params=_params("parallel", "arbitrary"),
    )(h, w)


def _qk_kernel(h_ref, w_ref, gain_ref, c_ref, s_ref, perm_ref, *outs, tn, sub):
    h = h_ref[...]
    gain = gain_ref[...]
    reps = sub // LANES
    cos = jnp.concatenate([c_ref[...]] * reps, axis=1)
    sin = jnp.concatenate([s_ref[...]] * reps, axis=1)
    perm = perm_ref[...]
    acc = jnp.dot(h, w_ref[...], preferred_element_type=F32)
    for s0 in range(0, tn, sub):
        xn = jnp.concatenate([_rms(acc[:, c0:c0 + LANES], gain) for c0 in range(s0, s0 + sub, LANES)], axis=1)
        hi = xn.astype(BF16)
        lo = (xn - hi.astype(F32)).astype(BF16)
        partner = jnp.dot(jnp.concatenate([hi, lo], axis=1), perm, preferred_element_type=F32)
        y = xn * cos + partner * sin
        for o in outs:
            o[:, s0:s0 + sub] = y.astype(o.dtype)


def _rope_perm(hd, rot, sub):
    half = rot // 2
    p = np.zeros((hd, hd), np.float32)
    for l in range(half):
        p[l + half, l] = -1.0
        p[l, l + half] = 1.0
    bd = np.kron(np.eye(sub // hd, dtype=np.float32), p)
    return np.concatenate([bd, bd], axis=0)


def _qk_proj(h, w, col0, ncols, gain, tables, dtypes, *, rot, tm=1024, tn=512):
    n, d = h.shape
    tm = _tile(n, tm)
    tn = _tile(ncols, tn)
    cos, sin = tables
    hd = cos.shape[1]
    tm = _tile(cos.shape[0], tm)
    nt = cos.shape[0] // tm
    off = col0 // tn
    sub = min(tn, MXU_COLS)
    perm = jnp.asarray(_rope_perm(hd, rot, sub), BF16)
    tspec = pl.BlockSpec((tm, hd), lambda i, j: (i % nt, 0))
    return pl.pallas_call(
        functools.partial(_qk_kernel, tn=tn, sub=sub),
        grid=(n // tm, ncols // tn),
        in_specs=[
            pl.BlockSpec((tm, d), lambda i, j: (i, 0)),
            pl.BlockSpec((d, tn), lambda i, j: (0, off + j)),
            pl.BlockSpec((1, hd), lambda i, j: (0, 0)),
            tspec, tspec,
            pl.BlockSpec(perm.shape, lambda i, j: (0, 0)),
        ],
        out_specs=[pl.BlockSpec((tm, tn), lambda i, j: (i, j)) for _ in dtypes],
        out_shape=[jax.ShapeDtypeStruct((n, ncols), dt) for dt in dtypes],
        compiler_params=_params("parallel", "arbitrary"),
    )(h, w, gain, cos, sin, perm)


def _rope_tables(pos, hd, rot):
    half = rot // 2
    inv = ROPE_THETA ** (-jnp.arange(half, dtype=F32) * 2.0 / rot)
    ang = pos.astype(F32)[:, None] * inv[None, :]
    cos, sin = jnp.cos(ang), jnp.sin(ang)
    n = pos.shape[0]
    c = jnp.concatenate([cos, cos, jnp.ones((n, hd - rot), F32)], axis=1)
    s = jnp.concatenate([sin, sin, jnp.zeros((n, hd - rot), F32)], axis=1)
    return c, s


def _lambda(lam_ref, li):
    lv = lam_ref[...]
    a = jnp.sum(lv[0:1] * lv[1:2], axis=-1, keepdims=True)
    b = jnp.sum(lv[2:3] * lv[3:4], axis=-1, keepdims=True)
    return jnp.exp(a) - jnp.exp(b) + li


def _pattn_kernel(q_ref, k_ref, v_ref, lam_ref, gn_ref, o_ref, *, t, nq, hd, li):
    c1 = hd ** -0.5 * math.log2(math.e)
    lam = _lambda(lam_ref, li)
    row = lax.broadcasted_iota(jnp.int32, (t, t), 0)
    col = lax.broadcasted_iota(jnp.int32, (t, t), 1)

    for n in range(nq):
        past = n * t
        q = q_ref[past:past + t, :]
        k = k_ref[0:past + t, :]
        e, inv = [], []
        for c in range(2):
            s = lax.dot_general(q[:, c * hd:(c + 1) * hd], k[:, c * hd:(c + 1) * hd], NT_DIMS,
                                preferred_element_type=F32)
            diag = jnp.where(row >= col, s[:, past:], -jnp.inf)
            s = jnp.concatenate([s[:, :past], diag], axis=1) if past else diag
            ec = jnp.exp2(s * c1 - jnp.max(s, axis=-1, keepdims=True) * c1)
            e.append(ec)
            inv.append(1.0 / jnp.sum(ec, axis=-1, keepdims=True))
        w = (e[0] * inv[0] - e[1] * (lam * inv[1])).astype(BF16)
        o = jnp.dot(w, v_ref[0:past + t, :], preferred_element_type=F32)
        o_ref[past:past + t, :] = (_rms(o, gn_ref[...]) * (1.0 - li)).astype(o_ref.dtype)


def _prompt_attention(q, k, v, lam_vecs, gn, *, nb, seq, hd, li, t=256):
    n, width = q.shape
    heads = width // (2 * hd)
    vd = v.shape[1] // heads
    t = _tile(seq, t)
    return pl.pallas_call(
        functools.partial(_pattn_kernel, t=t, nq=seq // t, hd=hd, li=li),
        grid=(nb, heads),
        in_specs=[
            pl.BlockSpec((seq, 2 * hd), lambda b, h: (b, h)),
            pl.BlockSpec((seq, 2 * hd), lambda b, h: (b, h)),
            pl.BlockSpec((seq, vd), lambda b, h: (b, h)),
            pl.BlockSpec((4, hd), lambda b, h: (0, 0)),
            pl.BlockSpec((1, vd), lambda b, h: (0, 0)),
        ],
        out_specs=pl.BlockSpec((seq, vd), lambda b, h: (b, h)),
        out_shape=jax.ShapeDtypeStruct((n, heads * vd), BF16),
        compiler_params=_params("parallel", "parallel"),
    )(q, k, v, lam_vecs, gn)


def _sattn_kernel(pt_ref, q_ref, kn_ref, vn_ref, lam_ref, gn_ref, *rest, G, heads, hd, ld, li):
    kp = rest[:G]
    vp = rest[G:2 * G]
    o_ref, qm_ref, m_ref, l_ref, acc_ref = rest[2 * G:]
    p = pl.program_id(1)
    scale = hd ** -0.5
    w2 = 2 * hd
    vd = acc_ref.shape[1]
    rows = 2 * ld
    page = kp[0].shape[1]
    pcols = page * heads

    @pl.when(p == 0)
    def _():
        row = lax.broadcasted_iota(jnp.int32, (rows, w2), 0)
        lane = lax.broadcasted_iota(jnp.int32, (rows, w2), 1)
        own_map = (row < ld) == (lane < hd)
        qi = lax.broadcasted_iota(jnp.int32, (rows, 1), 0) % ld
        qm, m0, l0, a0 = [], [], [], []
        for h in range(heads):
            qh = q_ref[0][:, h * w2:(h + 1) * w2]
            qf = jnp.where(own_map, jnp.concatenate([qh, qh], axis=0), 0.0)
            kn = kn_ref[0][:, h * w2:(h + 1) * w2]
            vn = vn_ref[0][:, h * vd:(h + 1) * vd]
            s = [jnp.where(qi >= j, jnp.sum(qf * kn[j:j + 1, :], axis=-1, keepdims=True) * scale, -jnp.inf)
                 for j in range(ld)]
            m = functools.reduce(jnp.maximum, s)
            pj = [jnp.exp(sj - m) for sj in s]
            qm.append(qf)
            m0.append(m)
            l0.append(functools.reduce(jnp.add, pj))
            a0.append(functools.reduce(jnp.add, [pj[j] * vn[j:j + 1, :] for j in range(ld)]))
        qm_ref[...] = jnp.concatenate(qm, axis=0).astype(BF16)
        m_ref[...] = jnp.concatenate(m0, axis=0)
        l_ref[...] = jnp.concatenate(l0, axis=0)
        acc_ref[...] = jnp.concatenate(a0, axis=0)

    nr = heads * rows
    qmat = qm_ref[...]
    same_head = (lax.broadcasted_iota(jnp.int32, (nr, pcols), 0) // rows
                 == lax.broadcasted_iota(jnp.int32, (nr, pcols), 1) % heads)
    s = jnp.concatenate(
        [jnp.where(same_head,
                   lax.dot_general(qmat, kp[g][0].reshape(pcols, w2).astype(BF16), NT_DIMS,
                                   preferred_element_type=F32) * scale, -jnp.inf)
         for g in range(G)], axis=1)
    m_old = m_ref[...]
    m_new = jnp.maximum(m_old, jnp.max(s, axis=-1, keepdims=True))
    alpha = jnp.exp(m_old - m_new)
    pr = jnp.exp(s - m_new)
    pv = functools.reduce(jnp.add, [
        jnp.dot(pr[:, g * pcols:(g + 1) * pcols].astype(BF16), vp[g][0].reshape(pcols, vd).astype(BF16),
                preferred_element_type=F32) for g in range(G)])
    m_ref[...] = m_new
    l_ref[...] = alpha * l_ref[...] + jnp.sum(pr, axis=-1, keepdims=True)
    acc_ref[...] = alpha * acc_ref[...] + pv

    @pl.when(p == pl.num_programs(1) - 1)
    def _():
        lam = _lambda(lam_ref, li)
        w = acc_ref[...] / l_ref[...]
        for h in range(heads):
            o = w[h * rows:h * rows + ld] - lam * w[h * rows + ld:(h + 1) * rows]
            o_ref[0, :, h * vd:(h + 1) * vd] = _rms(o, gn_ref[...]) * (1.0 - li)


def _sample_attention(q, kn, vn, cache_k, cache_v, page_table, lam_vecs, gn, *, pool0, hd, li, G=8):
    nb, ld, width = q.shape
    _, page, heads, w2 = cache_k.shape
    vwidth = vn.shape[2]
    vd = vwidth // heads
    n_pages = page_table.shape[1]
    G = _tile(n_pages, G)

    def page_spec(w, g):
        return pl.BlockSpec((1, page, heads, w), lambda b, p, pt: (pool0 + pt[b, p * G + g], 0, 0, 0))

    grid_spec = pltpu.PrefetchScalarGridSpec(
        num_scalar_prefetch=1,
        grid=(nb, n_pages // G),
        in_specs=[
            pl.BlockSpec((1, ld, width), lambda b, p, pt: (b, 0, 0)),
            pl.BlockSpec((1, ld, width), lambda b, p, pt: (b, 0, 0)),
            pl.BlockSpec((1, ld, vwidth), lambda b, p, pt: (b, 0, 0)),
            pl.BlockSpec((4, hd), lambda b, p, pt: (0, 0)),
            pl.BlockSpec((1, vd), lambda b, p, pt: (0, 0)),
        ] + [page_spec(w2, g) for g in range(G)] + [page_spec(vd, g) for g in range(G)],
        out_specs=pl.BlockSpec((1, ld, vwidth), lambda b, p, pt: (b, 0, 0)),
        scratch_shapes=[
            pltpu.VMEM((heads * 2 * ld, w2), BF16),
            pltpu.VMEM((heads * 2 * ld, 1), F32),
            pltpu.VMEM((heads * 2 * ld, 1), F32),
            pltpu.VMEM((heads * 2 * ld, vd), F32),
        ],
    )
    return pl.pallas_call(
        functools.partial(_sattn_kernel, G=G, heads=heads, hd=hd, ld=ld, li=li),
        grid_spec=grid_spec,
        out_shape=jax.ShapeDtypeStruct((nb, ld, vwidth), F32),
        compiler_params=_params("parallel", "arbitrary"),
    )(page_table, q, kn, vn, lam_vecs, gn, *([cache_k] * G), *([cache_v] * G))


def _gla_tables(c):
    t = np.arange(c)[:, None]
    u = np.arange(c)[None, :]
    mats = [u <= t, u > t]
    masks = []
    m = c // 2
    while m >= 1:
        mats.append((u > (t // m) * m) & (u <= t))
        mats.append((u > t) & (u <= (t // m + 1) * m))
        masks.append(((t // m) % 2 == 1) & ((u // m) == (t // m) - 1))
        m //= 2
    masks.append(t == u)
    return (np.concatenate(mats, 0).astype(np.float32), np.stack(masks, 0).astype(np.float32))


def _gla_kernel(q_ref, k_ref, v_ref, gd_ref, r_ref, s0_ref, wg_ref, bg_ref, gn_ref, mat_ref, mask_ref,
                o_ref, sout_ref, s_ref, *, c, valid, qscale):
    n = pl.program_id(1)
    heads, dk, dv = s_ref.shape

    @pl.when(n == 0)
    def _():
        s_ref[...] = s0_ref[0]

    mats = mat_ref[...]
    nlev = mask_ref.shape[0] - 1
    hs = range(heads)
    ks = [slice(h * dk, (h + 1) * dk) for h in hs]
    vs = [slice(h * dv, (h + 1) * dv) for h in hs]

    x = jnp.dot(gd_ref[...].astype(BF16), wg_ref[...], preferred_element_type=F32) + bg_ref[...]
    logg = (jnp.minimum(x, 0.0) - jnp.log1p(jnp.exp(-jnp.abs(x)))) * (1.0 / GATE_TAU)
    if valid < c:
        logg = jnp.where(lax.broadcasted_iota(jnp.int32, logg.shape, 0) < valid, logg, 0.0)
    hi = logg.astype(BF16)
    lo = (logg - hi.astype(F32)).astype(BF16)
    e = jnp.exp(jnp.dot(mats, hi, preferred_element_type=F32)
                + jnp.dot(mats, lo, preferred_element_type=F32))

    q = q_ref[...] * qscale
    k = k_ref[...]
    v = v_ref[...].astype(BF16)
    qb = q.astype(BF16)
    kb = k.astype(BF16)
    q_in = (q * e[0:c]).astype(BF16)
    k_end = (k * e[c:2 * c]).astype(BF16)
    s_old = [s_ref[h] for h in hs]

    o = [jnp.dot(q_in[:, ks[h]], s_old[h].astype(BF16), preferred_element_type=F32) for h in hs]
    a = [mask_ref[nlev] * lax.dot_general(qb[:, ks[h]], kb[:, ks[h]], NT_DIMS, preferred_element_type=F32)
         for h in hs]
    for lv in range(nlev):
        qt = (q * e[(2 + 2 * lv) * c:(3 + 2 * lv) * c]).astype(BF16)
        kt = (k * e[(3 + 2 * lv) * c:(4 + 2 * lv) * c]).astype(BF16)
        a = [a[h] + mask_ref[lv] * lax.dot_general(qt[:, ks[h]], kt[:, ks[h]], NT_DIMS,
                                                   preferred_element_type=F32) for h in hs]
    o = [o[h] + jnp.dot(a[h].astype(BF16), v[:, vs[h]], preferred_element_type=F32) for h in hs]
    upd = [lax.dot_general(k_end[:, ks[h]], v[:, vs[h]], TN_DIMS, preferred_element_type=F32) for h in hs]

    decay_cols = jnp.broadcast_to(e[c - 1:c], (LANES, heads * dk)).T
    for h in hs:
        decay = jnp.concatenate([decay_cols[ks[h]]] * (dv // LANES), axis=1)
        s_ref[h] = decay * s_old[h] + upd[h]
        r = r_ref[:, vs[h]]
        o_ref[:, vs[h]] = (_rms(o[h], gn_ref[...]) * (r * jax.nn.sigmoid(r))).astype(o_ref.dtype)

    @pl.when(n == pl.num_programs(1) - 1)
    def _():
        sout_ref[0] = s_ref[...]


def _gla(z, zcols, gd, s0, batch0, wg, bg, gn, *, nb, seq, c, valid, out_dtype):
    _, heads, dk, dv = s0.shape
    rank = gd.shape[1]
    nc = seq // c
    wk, wv = heads * dk, heads * dv
    qc, kc, vc, rc = zcols
    assert qc % wk == 0 and kc % wk == 0 and vc % wv == 0 and rc % wv == 0
    mats, masks = _gla_tables(c)
    mats = jnp.asarray(mats, BF16)
    masks = jnp.asarray(masks, F32)

    def zspec(w, col0):
        return pl.BlockSpec((c, w), lambda b, n: (b * nc + n, col0 // w))

    return pl.pallas_call(
        functools.partial(_gla_kernel, c=c, valid=valid, qscale=dk ** -0.5),
        grid=(nb, nc),
        in_specs=[
            zspec(wk, qc), zspec(wk, kc), zspec(wv, vc),
            pl.BlockSpec((c, rank), lambda b, n: (b * nc + n, 0)),
            zspec(wv, rc),
            pl.BlockSpec((1, heads, dk, dv), lambda b, n: (batch0 + b, 0, 0, 0)),
            pl.BlockSpec((rank, wk), lambda b, n: (0, 0)),
            pl.BlockSpec((1, wk), lambda b, n: (0, 0)),
            pl.BlockSpec((1, dv), lambda b, n: (0, 0)),
            pl.BlockSpec(mats.shape, lambda b, n: (0, 0)),
            pl.BlockSpec(masks.shape, lambda b, n: (0, 0, 0)),
        ],
        out_specs=[
            pl.BlockSpec((c, wv), lambda b, n: (b * nc + n, 0)),
            pl.BlockSpec((1, heads, dk, dv), lambda b, n: (b, 0, 0, 0)),
        ],
        out_shape=[
            jax.ShapeDtypeStruct((nb * seq, wv), out_dtype),
            jax.ShapeDtypeStruct((nb, heads, dk, dv), F32),
        ],
        scratch_shapes=[pltpu.VMEM((heads, dk, dv), F32)],
        compiler_params=_params("parallel", "arbitrary"),
    )(z, z, z, gd, z, s0, wg, bg, gn, mats, masks)


def _oproj_kernel(oa_ref, og_ref, wa_ref, wg_ref, x_ref, y_ref):
    acc = jnp.dot(oa_ref[...], wa_ref[...], preferred_element_type=F32)
    acc = acc + jnp.dot(og_ref[...], wg_ref[...], preferred_element_type=F32)
    y_ref[...] = x_ref[...] + acc


def _oproj(oa, og, w, x, *, tm=1024, tn=512):
    n, d = x.shape
    wa_rows = oa.shape[1]
    wg_rows = og.shape[1]
    assert wa_rows == wg_rows
    tm = _tile(n, tm)
    tn = _tile(d, tn)
    return pl.pallas_call(
        _oproj_kernel,
        grid=(n // tm, d // tn),
        in_specs=[
            pl.BlockSpec((tm, wa_rows), lambda i, j: (i, 0)),
            pl.BlockSpec((tm, wg_rows), lambda i, j: (i, 0)),
            pl.BlockSpec((wa_rows, tn), lambda i, j: (0, j)),
            pl.BlockSpec((wg_rows, tn), lambda i, j: (1, j)),
            pl.BlockSpec((tm, tn), lambda i, j: (i, j)),
        ],
        out_specs=pl.BlockSpec((tm, tn), lambda i, j: (i, j)),
        out_shape=jax.ShapeDtypeStruct((n, d), F32),
        compiler_params=_params("parallel", "arbitrary"),
    )(oa, og, w, w, x)


def _row(v):
    return v.reshape(1, -1).astype(F32)


def kernel(x_prompt, x_sample, cache_k, cache_v, state_gla, page_table, w_in, attn_q_norm, attn_k_norm, lam_q1, lam_k1, lam_q2, lam_k2, attn_out_norm, gla_w_gate_up, gla_b_gate, gla_out_norm, w_out, ffn1_norm, ffn1_w_gate, ffn1_w_up, ffn1_w_down, mix_norm, ffn2_norm, ffn2_w_gate, ffn2_w_up, ffn2_w_down, final_norm):
    nbp, seq, d = x_prompt.shape
    nbs, ld, _ = x_sample.shape
    depth, n_pool, page, heads_a, w2 = cache_k.shape
    hd = w2 // 2
    vd = cache_v.shape[-1]
    _, _, heads_g, dk, dv = state_gla.shape
    rank = gla_w_gate_up.shape[1]
    past = page_table.shape[1] * page
    rot = hd // 4
    wq = heads_a * w2
    wv = heads_a * vd
    wgq = heads_g * dk
    wgv = heads_g * dv
    c_q, c_k, c_v = 0, wq, 2 * wq
    c_rest = 2 * wq + wv
    w_rest = 2 * wgq + 2 * wgv
    c_gd = c_rest + w_rest
    zcols = (0, wgq, 2 * wgq, 2 * wgq + wgv)

    tab_p = _rope_tables(jnp.arange(seq), hd, rot)
    tab_s = _rope_tables(past + jnp.arange(nbs * ld) % ld, hd, rot)
    chunk = math.gcd(seq, GLA_CHUNK)
    ld_pad = -(-ld // 8) * 8

    xp = x_prompt.reshape(nbp * seq, d)
    xs = x_sample.reshape(nbs * ld, d)
    cache_k4 = cache_k.reshape(depth * n_pool, page, heads_a, w2)
    cache_v4 = cache_v.reshape(depth * n_pool, page, heads_a, vd)
    state4 = state_gla.reshape(depth * nbs, heads_g, dk, dv)
    zero_state = jnp.zeros((nbp, heads_g, dk, dv), F32)
    outs = [[] for _ in range(6)]
    for l in range(depth):
        li = 0.8 - 0.6 * math.exp(-0.3 * l)
        ffn1_w = (ffn1_w_gate[l], ffn1_w_up[l], ffn1_w_down[l])
        ffn2_w = (ffn2_w_gate[l], ffn2_w_up[l], ffn2_w_down[l])
        sample_casts = nbs * ld <= FFN_ROWS
        if not sample_casts:
            ffn1_w = tuple(w.astype(BF16) for w in ffn1_w)
            ffn2_w = tuple(w.astype(BF16) for w in ffn2_w)
        w_in_b = w_in[l].astype(BF16)
        w_gd = w_in_b[:, c_gd:]
        w_out_b = w_out[l].astype(BF16)
        lam_vecs = jnp.stack([lam_q1[l], lam_k1[l], lam_q2[l], lam_k2[l]]).astype(F32)
        gq, gk, gn_a = _row(attn_q_norm[l]), _row(attn_k_norm[l]), _row(attn_out_norm[l])
        wg = gla_w_gate_up[l].astype(BF16)
        bg, gn_g = _row(gla_b_gate[l]), _row(gla_out_norm[l])

        def ffn(x, norm, w, post, final, emit):
            res = _ffn(x, _row(norm), *w, _row(post), final=final, emit_bf16=emit)
            return res if emit else (res, w)

        def dense_in(x, tables, q_dtype, w, emit):
            x1, w = ffn(x, ffn1_norm[l], w, ffn1_norm[l], False, emit)
            h = _norm(x1, _row(mix_norm[l]))
            (qa,) = _qk_proj(h, w_in_b, c_q, wq, gq, tables, [q_dtype], rot=rot)
            ka, ka_b = _qk_proj(h, w_in_b, c_k, wq, gk, tables, [F32, BF16], rot=rot)
            va, va_b = _proj(h, w_in_b, c_v, wv, [F32, BF16])
            (z,) = _proj(h, w_in_b, c_rest, w_rest, [F32])
            (gd,) = _proj(h, w_gd, 0, rank, [F32])
            return (x1, qa, ka, ka_b, va, va_b, z, gd), w

        def dense_out(x1, oa, og, w, emit):
            x2 = _oproj(oa, og, w_out_b, x1)
            return ffn(x2, ffn2_norm[l], w, final_norm[l], True, emit)

        (x1, qa, ka, ka_b, va, va_b, z, gd), ffn1_w = dense_in(xs, tab_s, F32, ffn1_w, sample_casts)
        oa = _sample_attention(qa.reshape(nbs, ld, wq), ka.reshape(nbs, ld, wq), va.reshape(nbs, ld, wv),
                               cache_k4, cache_v4, page_table, lam_vecs, gn_a, pool0=l * n_pool, hd=hd, li=li)
        pad = lambda a: jnp.pad(a.reshape(nbs, ld, -1), ((0, 0), (0, ld_pad - ld), (0, 0))).reshape(nbs * ld_pad, -1)
        og, s_s = _gla(pad(z), zcols, pad(gd), state4, l * nbs, wg, bg, gn_g,
                       nb=nbs, seq=ld_pad, c=ld_pad, valid=ld, out_dtype=F32)
        og = og.reshape(nbs, ld_pad, wgv)[:, :ld].reshape(nbs * ld, wgv)
        xs, ffn2_w = dense_out(x1, oa.reshape(nbs * ld, wv).astype(BF16), og.astype(BF16), ffn2_w, sample_casts)
        outs[3].append(ka.reshape(nbs, ld, heads_a, w2))
        outs[4].append(va.reshape(nbs, ld, heads_a, vd))
        outs[5].append(s_s)

        (x1, qa, ka, ka_b, va, va_b, z, gd), _ = dense_in(xp, tab_p, BF16, ffn1_w, False)
        oa = _prompt_attention(qa, ka_b, va_b, lam_vecs, gn_a, nb=nbp, seq=seq, hd=hd, li=li)
        og, s_p = _gla(z, zcols, gd, zero_state, 0, wg, bg, gn_g,
                       nb=nbp, seq=seq, c=chunk, valid=chunk, out_dtype=BF16)
        xp, _ = dense_out(x1, oa, og, ffn2_w, False)
        outs[0].append(ka.reshape(nbp, seq, heads_a, w2))
        outs[1].append(va.reshape(nbp, seq, heads_a, vd))
        outs[2].append(s_p)

    return (xp.reshape(nbp, seq, d), xs.reshape(nbs, ld, d), jnp.stack(outs[0]), jnp.stack(outs[1]),
            jnp.stack(outs[2]), jnp.stack(outs[3]), jnp.stack(outs[4]), jnp.stack(outs[5]))
```

```python
import functools
import math

import numpy as np
import jax
import jax.numpy as jnp
from jax import lax
from jax.experimental import pallas as pl
from jax.experimental.pallas import tpu as pltpu

F32 = jnp.float32
BF16 = jnp.bfloat16
EPS = 1e-6
ROPE_THETA = 500000.0
GATE_TAU = 16.0
FFN_ROWS = 512
GLA_CHUNK = 128
LANES = 128
MXU_COLS = 256
VMEM_LIMIT = 60 * 1024 * 1024
NT_DIMS = (((1,), (1,)), ((), ()))
TN_DIMS = (((0,), (0,)), ((), ()))


def _params(*sem):
    return pltpu.CompilerParams(dimension_semantics=sem, vmem_limit_bytes=VMEM_LIMIT)


def _rms(x, gain):
    ms = jnp.mean(x * x, axis=-1, keepdims=True)
    return x * lax.rsqrt(ms + EPS) * gain


def _tile(n, want):
    t = min(n, want)
    while n % t:
        t //= 2
    return t


def _ffn_kernel(x_ref, g_ref, wg_ref, wu_ref, wd_ref, pg_ref, *rest, final, emit_bf16, tail):
    rest = list(rest)
    y_ref = rest.pop(0)
    wb_refs = [rest.pop(0) for _ in range(3)] if emit_bf16 else None
    (xn_ref,) = rest
    j = pl.program_id(1)
    last = pl.num_programs(1) - 1
    tf = wd_ref.shape[0]

    @pl.when(j == 0)
    def _():
        xn_ref[...] = _rms(x_ref[...], g_ref[...]).astype(BF16)
        y_ref[...] = jnp.zeros_like(y_ref)

    def accumulate(width):
        if emit_bf16:
            for ref, w_ref in zip(wb_refs, (wg_ref, wu_ref, wd_ref)):
                ref[...] = w_ref[...].astype(BF16)
            wg, wu, wd = (ref[...] for ref in wb_refs)
        else:
            wg, wu, wd = wg_ref[:, :width], wu_ref[:, :width], wd_ref[:width, :]
        xn = xn_ref[...]
        g = jnp.dot(xn, wg, preferred_element_type=F32)
        u = jnp.dot(xn, wu, preferred_element_type=F32)
        h = (g * jax.nn.sigmoid(g) * u).astype(BF16)
        y_ref[...] += jnp.dot(h, wd, preferred_element_type=F32)

    if tail == tf:
        accumulate(tf)
    else:
        pl.when(j < last)(functools.partial(accumulate, tf))
        pl.when(j == last)(functools.partial(accumulate, tail))

    @pl.when(j == last)
    def _():
        y = x_ref[...] + 0.5 * y_ref[...]
        y_ref[...] = _rms(y, pg_ref[...]) if final else y


def _ffn(x, norm_g, wg, wu, wd, post_g, *, final, emit_bf16, tm=FFN_ROWS, tf=512):
    n, d = x.shape
    dff = wd.shape[0]
    single = {}
    if emit_bf16:
        tf = _tile(dff, tf // 2)
        assert n <= tm
        single = dict(pipeline_mode=pl.Buffered(1))
    nf = pl.cdiv(dff, tf)
    tail = dff - (nf - 1) * tf
    tm = _tile(n, tm)
    out_shape = [jax.ShapeDtypeStruct((n, d), F32)]
    out_specs = [pl.BlockSpec((tm, d), lambda i, j: (i, 0), **single)]
    if emit_bf16:
        out_shape += [jax.ShapeDtypeStruct(w.shape, BF16) for w in (wg, wu, wd)]
        out_specs += [pl.BlockSpec((d, tf), lambda i, j: (0, j), **single),
                      pl.BlockSpec((d, tf), lambda i, j: (0, j), **single),
                      pl.BlockSpec((tf, d), lambda i, j: (j, 0), **single)]
    res = pl.pallas_call(
        functools.partial(_ffn_kernel, final=final, emit_bf16=emit_bf16, tail=tail),
        grid=(n // tm, nf),
        in_specs=[
            pl.BlockSpec((tm, d), lambda i, j: (i, 0), pipeline_mode=pl.Buffered(1)),
            pl.BlockSpec((1, d), lambda i, j: (0, 0)),
            pl.BlockSpec((d, tf), lambda i, j: (0, j)),
            pl.BlockSpec((d, tf), lambda i, j: (0, j)),
            pl.BlockSpec((tf, d), lambda i, j: (j, 0)),
            pl.BlockSpec((1, d), lambda i, j: (0, 0)),
        ],
        out_specs=out_specs,
        out_shape=out_shape,
        scratch_shapes=[pltpu.VMEM((tm, d), BF16)],
        compiler_params=_params("parallel", "arbitrary"),
    )(x, norm_g, wg, wu, wd, post_g)
    return (res[0], tuple(res[1:])) if emit_bf16 else res[0]


def _norm_kernel(x_ref, g_ref, o_ref):
    o_ref[...] = _rms(x_ref[...], g_ref[...]).astype(o_ref.dtype)


def _norm(x, gain, *, tm=FFN_ROWS):
    n, d = x.shape
    tm = _tile(n, tm)
    return pl.pallas_call(
        _norm_kernel,
        grid=(n // tm,),
        in_specs=[pl.BlockSpec((tm, d), lambda i: (i, 0)), pl.BlockSpec((1, d), lambda i: (0, 0))],
        out_specs=pl.BlockSpec((tm, d), lambda i: (i, 0)),
        out_shape=jax.ShapeDtypeStruct((n, d), BF16),
        compiler_params=_params("parallel"),
    )(x, gain)


def _proj_kernel(h_ref, w_ref, *outs):
    acc = jnp.dot(h_ref[...], w_ref[...], preferred_element_type=F32)
    for o in outs:
        o[...] = acc.astype(o.dtype)


def _proj(h, w, col0, ncols, dtypes, *, tm=1024, tn=1024):
    n, d = h.shape
    tm = _tile(n, tm)
    tn = _tile(ncols, tn)
    if ncols == w.shape[1]:
        off = 0
    else:
        assert col0 % tn == 0
        off = col0 // tn
    return pl.pallas_call(
        _proj_kernel,
        grid=(n // tm, ncols // tn),
        in_specs=[
            pl.BlockSpec((tm, d), lambda i, j: (i, 0)),
            pl.BlockSpec((d, tn), lambda i, j: (0, off + j)),
        ],
        out_specs=[pl.BlockSpec((tm, tn), lambda i, j: (i, j)) for _ in dtypes],
        out_shape=[jax.ShapeDtypeStruct((n, ncols), dt) for dt in dtypes],
        compiler_params=_params("parallel", "arbitrary"),
    )(h, w)


def _qk_kernel(h_ref, w_ref, gain_ref, c_ref, s_ref, perm_ref, *outs, tn, sub):
    h = h_ref[...]
    gain = gain_ref[...]
    reps = sub // LANES
    cos = jnp.concatenate([c_ref[...]] * reps, axis=1)
    sin = jnp.concatenate([s_ref[...]] * reps, axis=1)
    perm = perm_ref[...]
    acc = jnp.dot(h, w_ref[...], preferred_element_type=F32)
    for s0 in range(0, tn, sub):
        xn = jnp.concatenate([_rms(acc[:, c0:c0 + LANES], gain) for c0 in range(s0, s0 + sub, LANES)], axis=1)
        hi = xn.astype(BF16)
        lo = (xn - hi.astype(F32)).astype(BF16)
        partner = jnp.dot(jnp.concatenate([hi, lo], axis=1), perm, preferred_element_type=F32)
        y = xn * cos + partner * sin
        for o in outs:
            o[:, s0:s0 + sub] = y.astype(o.dtype)


def _rope_perm(hd, rot, sub):
    half = rot // 2
    p = np.zeros((hd, hd), np.float32)
    for l in range(half):
        p[l + half, l] = -1.0
        p[l, l + half] = 1.0
    bd = np.kron(np.eye(sub // hd, dtype=np.float32), p)
    return np.concatenate([bd, bd], axis=0)


def _qk_proj(h, w, col0, ncols, gain, tables, dtypes, *, rot, tm=1024, tn=1024):
    n, d = h.shape
    tm = _tile(n, tm)
    tn = _tile(ncols, tn)
    cos, sin = tables
    hd = cos.shape[1]
    tm = _tile(cos.shape[0], tm)
    nt = cos.shape[0] // tm
    off = col0 // tn
    sub = min(tn, MXU_COLS)
    perm = jnp.asarray(_rope_perm(hd, rot, sub), BF16)
    tspec = pl.BlockSpec((tm, hd), lambda i, j: (i % nt, 0))
    return pl.pallas_call(
        functools.partial(_qk_kernel, tn=tn, sub=sub),
        grid=(n // tm, ncols // tn),
        in_specs=[
            pl.BlockSpec((tm, d), lambda i, j: (i, 0)),
            pl.BlockSpec((d, tn), lambda i, j: (0, off + j)),
            pl.BlockSpec((1, hd), lambda i, j: (0, 0)),
            tspec, tspec,
            pl.BlockSpec(perm.shape, lambda i, j: (0, 0)),
        ],
        out_specs=[pl.BlockSpec((tm, tn), lambda i, j: (i, j)) for _ in dtypes],
        out_shape=[jax.ShapeDtypeStruct((n, ncols), dt) for dt in dtypes],
        compiler_params=_params("parallel", "arbitrary"),
    )(h, w, gain, cos, sin, perm)


def _rope_tables(pos, hd, rot):
    half = rot // 2
    inv = ROPE_THETA ** (-jnp.arange(half, dtype=F32) * 2.0 / rot)
    ang = pos.astype(F32)[:, None] * inv[None, :]
    cos, sin = jnp.cos(ang), jnp.sin(ang)
    n = pos.shape[0]
    c = jnp.concatenate([cos, cos, jnp.ones((n, hd - rot), F32)], axis=1)
    s = jnp.concatenate([sin, sin, jnp.zeros((n, hd - rot), F32)], axis=1)
    return c, s


def _lambda(lam_ref, li):
    lv = lam_ref[...]
    a = jnp.sum(lv[0:1] * lv[1:2], axis=-1, keepdims=True)
    b = jnp.sum(lv[2:3] * lv[3:4], axis=-1, keepdims=True)
    return jnp.exp(a) - jnp.exp(b) + li


def _pattn_kernel(q_ref, k_ref, v_ref, lam_ref, gn_ref, o_ref, *, t, nq, hd, li):
    c1 = hd ** -0.5 * math.log2(math.e)
    lam = _lambda(lam_ref, li)
    row = lax.broadcasted_iota(jnp.int32, (t, t), 0)
    col = lax.broadcasted_iota(jnp.int32, (t, t), 1)

    for n in range(nq):
        past = n * t
        q = q_ref[past:past + t, :]
        k = k_ref[0:past + t, :]
        e, inv = [], []
        for c in range(2):
            s = lax.dot_general(q[:, c * hd:(c + 1) * hd], k[:, c * hd:(c + 1) * hd], NT_DIMS,
                                preferred_element_type=F32)
            diag = jnp.where(row >= col, s[:, past:], -jnp.inf)
            s = jnp.concatenate([s[:, :past], diag], axis=1) if past else diag
            ec = jnp.exp2(s * c1 - jnp.max(s, axis=-1, keepdims=True) * c1)
            e.append(ec)
            inv.append(1.0 / jnp.sum(ec, axis=-1, keepdims=True))
        w = (e[0] * inv[0] - e[1] * (lam * inv[1])).astype(BF16)
        o = jnp.dot(w, v_ref[0:past + t, :], preferred_element_type=F32)
        o_ref[past:past + t, :] = (_rms(o, gn_ref[...]) * (1.0 - li)).astype(o_ref.dtype)


def _prompt_attention(q, k, v, lam_vecs, gn, *, nb, seq, hd, li, t=256):
    n, width = q.shape
    heads = width // (2 * hd)
    vd = v.shape[1] // heads
    t = _tile(seq, t)
    return pl.pallas_call(
        functools.partial(_pattn_kernel, t=t, nq=seq // t, hd=hd, li=li),
        grid=(nb, heads),
        in_specs=[
            pl.BlockSpec((seq, 2 * hd), lambda b, h: (b, h)),
            pl.BlockSpec((seq, 2 * hd), lambda b, h: (b, h)),
            pl.BlockSpec((seq, vd), lambda b, h: (b, h)),
            pl.BlockSpec((4, hd), lambda b, h: (0, 0)),
            pl.BlockSpec((1, vd), lambda b, h: (0, 0)),
        ],
        out_specs=pl.BlockSpec((seq, vd), lambda b, h: (b, h)),
        out_shape=jax.ShapeDtypeStruct((n, heads * vd), BF16),
        compiler_params=_params("parallel", "parallel"),
    )(q, k, v, lam_vecs, gn)


def _sattn_kernel(pt_ref, q_ref, kn_ref, vn_ref, lam_ref, gn_ref, *rest, G, heads, hd, ld, li):
    kp = rest[:G]
    vp = rest[G:2 * G]
    o_ref, qm_ref, m_ref, l_ref, acc_ref = rest[2 * G:]
    p = pl.program_id(1)
    scale = hd ** -0.5
    w2 = 2 * hd
    vd = acc_ref.shape[1]
    rows = 2 * ld
    page = kp[0].shape[1]
    pcols = page * heads

    @pl.when(p == 0)
    def _():
        row = lax.broadcasted_iota(jnp.int32, (rows, w2), 0)
        lane = lax.broadcasted_iota(jnp.int32, (rows, w2), 1)
        own_map = (row < ld) == (lane < hd)
        qi = lax.broadcasted_iota(jnp.int32, (rows, 1), 0) % ld
        qm, m0, l0, a0 = [], [], [], []
        for h in range(heads):
            qh = q_ref[0][:, h * w2:(h + 1) * w2]
            qf = jnp.where(own_map, jnp.concatenate([qh, qh], axis=0), 0.0)
            kn = kn_ref[0][:, h * w2:(h + 1) * w2]
            vn = vn_ref[0][:, h * vd:(h + 1) * vd]
            s = [jnp.where(qi >= j, jnp.sum(qf * kn[j:j + 1, :], axis=-1, keepdims=True) * scale, -jnp.inf)
                 for j in range(ld)]
            m = functools.reduce(jnp.maximum, s)
            pj = [jnp.exp(sj - m) for sj in s]
            qm.append(qf)
            m0.append(m)
            l0.append(functools.reduce(jnp.add, pj))
            a0.append(functools.reduce(jnp.add, [pj[j] * vn[j:j + 1, :] for j in range(ld)]))
        qm_ref[...] = jnp.concatenate(qm, axis=0).astype(BF16)
        m_ref[...] = jnp.concatenate(m0, axis=0)
        l_ref[...] = jnp.concatenate(l0, axis=0)
        acc_ref[...] = jnp.concatenate(a0, axis=0)

    nr = heads * rows
    qmat = qm_ref[...]
    same_head = (lax.broadcasted_iota(jnp.int32, (nr, pcols), 0) // rows
                 == lax.broadcasted_iota(jnp.int32, (nr, pcols), 1) % heads)
    s = jnp.concatenate(
        [jnp.where(same_head,
                   lax.dot_general(qmat, kp[g][0].reshape(pcols, w2).astype(BF16), NT_DIMS,
                                   preferred_element_type=F32) * scale, -jnp.inf)
         for g in range(G)], axis=1)
    m_old = m_ref[...]
    m_new = jnp.maximum(m_old, jnp.max(s, axis=-1, keepdims=True))
    alpha = jnp.exp(m_old - m_new)
    pr = jnp.exp(s - m_new)
    pv = functools.reduce(jnp.add, [
        jnp.dot(pr[:, g * pcols:(g + 1) * pcols].astype(BF16), vp[g][0].reshape(pcols, vd).astype(BF16),
                preferred_element_type=F32) for g in range(G)])
    m_ref[...] = m_new
    l_ref[...] = alpha * l_ref[...] + jnp.sum(pr, axis=-1, keepdims=True)
    acc_ref[...] = alpha * acc_ref[...] + pv

    @pl.when(p == pl.num_programs(1) - 1)
    def _():
        lam = _lambda(lam_ref, li)
        w = acc_ref[...] / l_ref[...]
        for h in range(heads):
            o = w[h * rows:h * rows + ld] - lam * w[h * rows + ld:(h + 1) * rows]
            o_ref[0, :, h * vd:(h + 1) * vd] = _rms(o, gn_ref[...]) * (1.0 - li)


def _sample_attention(q, kn, vn, cache_k, cache_v, page_table, lam_vecs, gn, *, pool0, hd, li, G=8):
    nb, ld, width = q.shape
    _, page, heads, w2 = cache_k.shape
    vwidth = vn.shape[2]
    vd = vwidth // heads
    n_pages = page_table.shape[1]
    G = _tile(n_pages, G)

    def page_spec(w, g):
        return pl.BlockSpec((1, page, heads, w), lambda b, p, pt: (pool0 + pt[b, p * G + g], 0, 0, 0))

    grid_spec = pltpu.PrefetchScalarGridSpec(
        num_scalar_prefetch=1,
        grid=(nb, n_pages // G),
        in_specs=[
            pl.BlockSpec((1, ld, width), lambda b, p, pt: (b, 0, 0)),
            pl.BlockSpec((1, ld, width), lambda b, p, pt: (b, 0, 0)),
            pl.BlockSpec((1, ld, vwidth), lambda b, p, pt: (b, 0, 0)),
            pl.BlockSpec((4, hd), lambda b, p, pt: (0, 0)),
            pl.BlockSpec((1, vd), lambda b, p, pt: (0, 0)),
        ] + [page_spec(w2, g) for g in range(G)] + [page_spec(vd, g) for g in range(G)],
        out_specs=pl.BlockSpec((1, ld, vwidth), lambda b, p, pt: (b, 0, 0)),
        scratch_shapes=[
            pltpu.VMEM((heads * 2 * ld, w2), BF16),
            pltpu.VMEM((heads * 2 * ld, 1), F32),
            pltpu.VMEM((heads * 2 * ld, 1), F32),
            pltpu.VMEM((heads * 2 * ld, vd), F32),
        ],
    )
    return pl.pallas_call(
        functools.partial(_sattn_kernel, G=G, heads=heads, hd=hd, ld=ld, li=li),
        grid_spec=grid_spec,
        out_shape=jax.ShapeDtypeStruct((nb, ld, vwidth), F32),
        compiler_params=_params("parallel", "arbitrary"),
    )(page_table, q, kn, vn, lam_vecs, gn, *([cache_k] * G), *([cache_v] * G))


def _gla_tables(c):
    t = np.arange(c)[:, None]
    u = np.arange(c)[None, :]
    mats = [u <= t, u > t]
    masks = []
    m = c // 2
    while m >= 1:
        mats.append((u > (t // m) * m) & (u <= t))
        mats.append((u > t) & (u <= (t // m + 1) * m))
        masks.append(((t // m) % 2 == 1) & ((u // m) == (t // m) - 1))
        m //= 2
    masks.append(t == u)
    return (np.concatenate(mats, 0).astype(np.float32), np.stack(masks, 0).astype(np.float32))


def _gla_kernel(q_ref, k_ref, v_ref, gd_ref, r_ref, s0_ref, wg_ref, bg_ref, gn_ref, mat_ref, mask_ref,
                o_ref, sout_ref, s_ref, *, c, valid, qscale):
    n = pl.program_id(1)
    heads, dk, dv = s_ref.shape

    @pl.when(n == 0)
    def _():
        s_ref[...] = s0_ref[0]

    mats = mat_ref[...]
    nlev = mask_ref.shape[0] - 1
    hs = range(heads)
    ks = [slice(h * dk, (h + 1) * dk) for h in hs]
    vs = [slice(h * dv, (h + 1) * dv) for h in hs]

    x = jnp.dot(gd_ref[...].astype(BF16), wg_ref[...], preferred_element_type=F32) + bg_ref[...]
    logg = (jnp.minimum(x, 0.0) - jnp.log1p(jnp.exp(-jnp.abs(x)))) * (1.0 / GATE_TAU)
    if valid < c:
        logg = jnp.where(lax.broadcasted_iota(jnp.int32, logg.shape, 0) < valid, logg, 0.0)
    hi = logg.astype(BF16)
    lo = (logg - hi.astype(F32)).astype(BF16)
    e = jnp.exp(jnp.dot(mats, hi, preferred_element_type=F32)
                + jnp.dot(mats, lo, preferred_element_type=F32))

    q = q_ref[...] * qscale
    k = k_ref[...]
    v = v_ref[...].astype(BF16)
    qb = q.astype(BF16)
    kb = k.astype(BF16)
    q_in = (q * e[0:c]).astype(BF16)
    k_end = (k * e[c:2 * c]).astype(BF16)
    s_old = [s_ref[h] for h in hs]

    o = [jnp.dot(q_in[:, ks[h]], s_old[h].astype(BF16), preferred_element_type=F32) for h in hs]
    a = [mask_ref[nlev] * lax.dot_general(qb[:, ks[h]], kb[:, ks[h]], NT_DIMS, preferred_element_type=F32)
         for h in hs]
    for lv in range(nlev):
        qt = (q * e[(2 + 2 * lv) * c:(3 + 2 * lv) * c]).astype(BF16)
        kt = (k * e[(3 + 2 * lv) * c:(4 + 2 * lv) * c]).astype(BF16)
        a = [a[h] + mask_ref[lv] * lax.dot_general(qt[:, ks[h]], kt[:, ks[h]], NT_DIMS,
                                                   preferred_element_type=F32) for h in hs]
    o = [o[h] + jnp.dot(a[h].astype(BF16), v[:, vs[h]], preferred_element_type=F32) for h in hs]
    upd = [lax.dot_general(k_end[:, ks[h]], v[:, vs[h]], TN_DIMS, preferred_element_type=F32) for h in hs]

    decay_cols = jnp.broadcast_to(e[c - 1:c], (LANES, heads * dk)).T
    for h in hs:
        decay = jnp.concatenate([decay_cols[ks[h]]] * (dv // LANES), axis=1)
        s_ref[h] = decay * s_old[h] + upd[h]
        r = r_ref[:, vs[h]]
        o_ref[:, vs[h]] = (_rms(o[h], gn_ref[...]) * (r * jax.nn.sigmoid(r))).astype(o_ref.dtype)

    @pl.when(n == pl.num_programs(1) - 1)
    def _():
        sout_ref[0] = s_ref[...]


def _gla(z, zcols, gd, s0, batch0, wg, bg, gn, *, nb, seq, c, valid, out_dtype):
    _, heads, dk, dv = s0.shape
    rank = gd.shape[1]
    nc = seq // c
    wk, wv = heads * dk, heads * dv
    qc, kc, vc, rc = zcols
    assert qc % wk == 0 and kc % wk == 0 and vc % wv == 0 and rc % wv == 0
    mats, masks = _gla_tables(c)
    mats = jnp.asarray(mats, BF16)
    masks = jnp.asarray(masks, F32)

    def zspec(w, col0):
        return pl.BlockSpec((c, w), lambda b, n: (b * nc + n, col0 // w))

    return pl.pallas_call(
        functools.partial(_gla_kernel, c=c, valid=valid, qscale=dk ** -0.5),
        grid=(nb, nc),
        in_specs=[
            zspec(wk, qc), zspec(wk, kc), zspec(wv, vc),
            pl.BlockSpec((c, rank), lambda b, n: (b * nc + n, 0)),
            zspec(wv, rc),
            pl.BlockSpec((1, heads, dk, dv), lambda b, n: (batch0 + b, 0, 0, 0)),
            pl.BlockSpec((rank, wk), lambda b, n: (0, 0)),
            pl.BlockSpec((1, wk), lambda b, n: (0, 0)),
            pl.BlockSpec((1, dv), lambda b, n: (0, 0)),
            pl.BlockSpec(mats.shape, lambda b, n: (0, 0)),
            pl.BlockSpec(masks.shape, lambda b, n: (0, 0, 0)),
        ],
        out_specs=[
            pl.BlockSpec((c, wv), lambda b, n: (b * nc + n, 0)),
            pl.BlockSpec((1, heads, dk, dv), lambda b, n: (b, 0, 0, 0)),
        ],
        out_shape=[
            jax.ShapeDtypeStruct((nb * seq, wv), out_dtype),
            jax.ShapeDtypeStruct((nb, heads, dk, dv), F32),
        ],
        scratch_shapes=[pltpu.VMEM((heads, dk, dv), F32)],
        compiler_params=_params("parallel", "arbitrary"),
    )(z, z, z, gd, z, s0, wg, bg, gn, mats, masks)


def _oproj_kernel(oa_ref, og_ref, wa_ref, wg_ref, x_ref, y_ref):
    acc = jnp.dot(oa_ref[...], wa_ref[...], preferred_element_type=F32)
    acc = acc + jnp.dot(og_ref[...], wg_ref[...], preferred_element_type=F32)
    y_ref[...] = x_ref[...] + acc


def _oproj(oa, og, w, x, *, tm=1024, tn=1024):
    n, d = x.shape
    wa_rows = oa.shape[1]
    wg_rows = og.shape[1]
    assert wa_rows == wg_rows
    tm = _tile(n, tm)
    tn = _tile(d, tn)
    return pl.pallas_call(
        _oproj_kernel,
        grid=(n // tm, d // tn),
        in_specs=[
            pl.BlockSpec((tm, wa_rows), lambda i, j: (i, 0)),
            pl.BlockSpec((tm, wg_rows), lambda i, j: (i, 0)),
            pl.BlockSpec((wa_rows, tn), lambda i, j: (0, j)),
            pl.BlockSpec((wg_rows, tn), lambda i, j: (1, j)),
            pl.BlockSpec((tm, tn), lambda i, j: (i, j)),
        ],
        out_specs=pl.BlockSpec((tm, tn), lambda i, j: (i, j)),
        out_shape=jax.ShapeDtypeStruct((n, d), F32),
        compiler_params=_params("parallel", "arbitrary"),
    )(oa, og, w, w, x)


def _row(v):
    return v.reshape(1, -1).astype(F32)


def kernel(x_prompt, x_sample, cache_k, cache_v, state_gla, page_table, w_in, attn_q_norm, attn_k_norm, lam_q1, lam_k1, lam_q2, lam_k2, attn_out_norm, gla_w_gate_up, gla_b_gate, gla_out_norm, w_out, ffn1_norm, ffn1_w_gate, ffn1_w_up, ffn1_w_down, mix_norm, ffn2_norm, ffn2_w_gate, ffn2_w_up, ffn2_w_down, final_norm):
    nbp, seq, d = x_prompt.shape
    nbs, ld, _ = x_sample.shape
    depth, n_pool, page, heads_a, w2 = cache_k.shape
    hd = w2 // 2
    vd = cache_v.shape[-1]
    _, _, heads_g, dk, dv = state_gla.shape
    rank = gla_w_gate_up.shape[1]
    past = page_table.shape[1] * page
    rot = hd // 4
    wq = heads_a * w2
    wv = heads_a * vd
    wgq = heads_g * dk
    wgv = heads_g * dv
    c_q, c_k, c_v = 0, wq, 2 * wq
    c_rest = 2 * wq + wv
    w_rest = 2 * wgq + 2 * wgv
    c_gd = c_rest + w_rest
    zcols = (0, wgq, 2 * wgq, 2 * wgq + wgv)

    tab_p = _rope_tables(jnp.arange(seq), hd, rot)
    tab_s = _rope_tables(past + jnp.arange(nbs * ld) % ld, hd, rot)
    chunk = math.gcd(seq, GLA_CHUNK)
    ld_pad = -(-ld // 8) * 8

    xp = x_prompt.reshape(nbp * seq, d)
    xs = x_sample.reshape(nbs * ld, d)
    cache_k4 = cache_k.reshape(depth * n_pool, page, heads_a, w2)
    cache_v4 = cache_v.reshape(depth * n_pool, page, heads_a, vd)
    state4 = state_gla.reshape(depth * nbs, heads_g, dk, dv)
    zero_state = jnp.zeros((nbp, heads_g, dk, dv), F32)
    outs = [[] for _ in range(6)]
    for l in range(depth):
        li = 0.8 - 0.6 * math.exp(-0.3 * l)
        ffn1_w = (ffn1_w_gate[l], ffn1_w_up[l], ffn1_w_down[l])
        ffn2_w = (ffn2_w_gate[l], ffn2_w_up[l], ffn2_w_down[l])
        sample_casts = nbs * ld <= FFN_ROWS
        if not sample_casts:
            ffn1_w = tuple(w.astype(BF16) for w in ffn1_w)
            ffn2_w = tuple(w.astype(BF16) for w in ffn2_w)
        w_in_b = w_in[l].astype(BF16)
        w_gd = w_in_b[:, c_gd:]
        w_out_b = w_out[l].astype(BF16)
        lam_vecs = jnp.stack([lam_q1[l], lam_k1[l], lam_q2[l], lam_k2[l]]).astype(F32)
        gq, gk, gn_a = _row(attn_q_norm[l]), _row(attn_k_norm[l]), _row(attn_out_norm[l])
        wg = gla_w_gate_up[l].astype(BF16)
        bg, gn_g = _row(gla_b_gate[l]), _row(gla_out_norm[l])

        def ffn(x, norm, w, post, final, emit):
            res = _ffn(x, _row(norm), *w, _row(post), final=final, emit_bf16=emit)
            return res if emit else (res, w)

        def dense_in(x, tables, q_dtype, w, emit):
            x1, w = ffn(x, ffn1_norm[l], w, ffn1_norm[l], False, emit)
            h = _norm(x1, _row(mix_norm[l]))
            (qa,) = _qk_proj(h, w_in_b, c_q, wq, gq, tables, [q_dtype], rot=rot)
            ka, ka_b = _qk_proj(h, w_in_b, c_k, wq, gk, tables, [F32, BF16], rot=rot)
            va, va_b = _proj(h, w_in_b, c_v, wv, [F32, BF16])
            (z,) = _proj(h, w_in_b, c_rest, w_rest, [F32])
            (gd,) = _proj(h, w_gd, 0, rank, [F32])
            return (x1, qa, ka, ka_b, va, va_b, z, gd), w

        def dense_out(x1, oa, og, w, emit):
            x2 = _oproj(oa, og, w_out_b, x1)
            return ffn(x2, ffn2_norm[l], w, final_norm[l], True, emit)

        (x1, qa, ka, ka_b, va, va_b, z, gd), ffn1_w = dense_in(xs, tab_s, F32, ffn1_w, sample_casts)
        oa = _sample_attention(qa.reshape(nbs, ld, wq), ka.reshape(nbs, ld, wq), va.reshape(nbs, ld, wv),
                               cache_k4, cache_v4, page_table, lam_vecs, gn_a, pool0=l * n_pool, hd=hd, li=li)
        pad = lambda a: jnp.pad(a.reshape(nbs, ld, -1), ((0, 0), (0, ld_pad - ld), (0, 0))).reshape(nbs * ld_pad, -1)
        og, s_s = _gla(pad(z), zcols, pad(gd), state4, l * nbs, wg, bg, gn_g,
                       nb=nbs, seq=ld_pad, c=ld_pad, valid=ld, out_dtype=F32)
        og = og.reshape(nbs, ld_pad, wgv)[:, :ld].reshape(nbs * ld, wgv)
        xs, ffn2_w = dense_out(x1, oa.reshape(nbs * ld, wv).astype(BF16), og.astype(BF16), ffn2_w, sample_casts)
        outs[3].append(ka.reshape(nbs, ld, heads_a, w2))
        outs[4].append(va.reshape(nbs, ld, heads_a, vd))
        outs[5].append(s_s)

        (x1, qa, ka, ka_b, va, va_b, z, gd), _ = dense_in(xp, tab_p, BF16, ffn1_w, False)
        oa = _prompt_attention(qa, ka_b, va_b, lam_vecs, gn_a, nb=nbp, seq=seq, hd=hd, li=li)
        og, s_p = _gla(z, zcols, gd, zero_state, 0, wg, bg, gn_g,
                       nb=nbp, seq=seq, c=chunk, valid=chunk, out_dtype=BF16)
        xp, _ = dense_out(x1, oa, og, ffn2_w, False)
        outs[0].append(ka.reshape(nbp, seq, heads_a, w2))
        outs[1].append(va.reshape(nbp, seq, heads_a, vd))
        outs[2].append(s_p)

    return (xp.reshape(nbp, seq, d), xs.reshape(nbs, ld, d), jnp.stack(outs[0]), jnp.stack(outs[1]),
            jnp.stack(outs[2]), jnp.stack(outs[3]), jnp.stack(outs[4]), jnp.stack(outs[5]))
```

```python
import functools
import math

import numpy as np
import jax
import jax.numpy as jnp
from jax import lax
from jax.experimental import pallas as pl
from jax.experimental.pallas import tpu as pltpu

F32 = jnp.float32
BF16 = jnp.bfloat16
EPS = 1e-6
ROPE_THETA = 500000.0
GATE_TAU = 16.0
FFN_ROWS = 512
GLA_CHUNK = 128
LANES = 128
MXU_COLS = 256
VMEM_LIMIT = 60 * 1024 * 1024
NT_DIMS = (((1,), (1,)), ((), ()))
TN_DIMS = (((0,), (0,)), ((), ()))


def _params(*sem):
    return pltpu.CompilerParams(dimension_semantics=sem, vmem_limit_bytes=VMEM_LIMIT)


def _rms(x, gain):
    ms = jnp.mean(x * x, axis=-1, keepdims=True)
    return x * lax.rsqrt(ms + EPS) * gain


def _tile(n, want):
    t = min(n, want)
    while n % t:
        t //= 2
    return t


def _ffn_kernel(x_ref, g_ref, wg_ref, wu_ref, wd_ref, pg_ref, *rest, final, emit_bf16, nf, tail):
    rest = list(rest)
    y_ref = rest.pop(0)
    wb_refs = [rest.pop(0) for _ in range(3)] if emit_bf16 else None
    (xn_ref,) = rest
    j = pl.program_id(1)
    last = nf - 1
    tf = wd_ref.shape[0]

    @pl.when(j == 0)
    def _():
        xn_ref[...] = _rms(x_ref[...], g_ref[...]).astype(BF16)

    def accumulate(width, first):
        if emit_bf16:
            for ref, w_ref in zip(wb_refs, (wg_ref, wu_ref, wd_ref)):
                ref[...] = w_ref[...].astype(BF16)
            wg, wu, wd = (ref[...] for ref in wb_refs)
        else:
            wg, wu, wd = wg_ref[:, :width], wu_ref[:, :width], wd_ref[:width, :]
        xn = xn_ref[...]
        g = jnp.dot(xn, wg, preferred_element_type=F32)
        u = jnp.dot(xn, wu, preferred_element_type=F32)
        h = (g * jax.nn.sigmoid(g) * u).astype(BF16)
        acc = jnp.dot(h, wd, preferred_element_type=F32)
        if first:
            y_ref[...] = acc
        else:
            y_ref[...] += acc

    if nf == 1:
        accumulate(tail, True)
    else:
        pl.when(j == 0)(functools.partial(accumulate, tf, True))
        if tail == tf:
            pl.when(j > 0)(functools.partial(accumulate, tf, False))
        else:
            pl.when((j > 0) & (j < last))(functools.partial(accumulate, tf, False))
            pl.when(j == last)(functools.partial(accumulate, tail, False))

    @pl.when(j == last)
    def _():
        y = x_ref[...] + 0.5 * y_ref[...]
        y_ref[...] = _rms(y, pg_ref[...]) if final else y


def _ffn(x, norm_g, wg, wu, wd, post_g, *, final, emit_bf16, tm=FFN_ROWS, tf=512):
    n, d = x.shape
    dff = wd.shape[0]
    single = {}
    if emit_bf16:
        tf = _tile(dff, tf // 2)
        assert n <= tm
        single = dict(pipeline_mode=pl.Buffered(1))
    nf = pl.cdiv(dff, tf)
    tail = dff - (nf - 1) * tf
    tm = _tile(n, tm)
    out_shape = [jax.ShapeDtypeStruct((n, d), F32)]
    out_specs = [pl.BlockSpec((tm, d), lambda i, j: (i, 0), **single)]
    if emit_bf16:
        out_shape += [jax.ShapeDtypeStruct(w.shape, BF16) for w in (wg, wu, wd)]
        out_specs += [pl.BlockSpec((d, tf), lambda i, j: (0, j), **single),
                      pl.BlockSpec((d, tf), lambda i, j: (0, j), **single),
                      pl.BlockSpec((tf, d), lambda i, j: (j, 0), **single)]
    res = pl.pallas_call(
        functools.partial(_ffn_kernel, final=final, emit_bf16=emit_bf16, nf=nf, tail=tail),
        grid=(n // tm, nf),
        in_specs=[
            pl.BlockSpec((tm, d), lambda i, j: (i, 0), pipeline_mode=pl.Buffered(1)),
            pl.BlockSpec((1, d), lambda i, j: (0, 0)),
            pl.BlockSpec((d, tf), lambda i, j: (0, j)),
            pl.BlockSpec((d, tf), lambda i, j: (0, j)),
            pl.BlockSpec((tf, d), lambda i, j: (j, 0)),
            pl.BlockSpec((1, d), lambda i, j: (0, 0)),
        ],
        out_specs=out_specs,
        out_shape=out_shape,
        scratch_shapes=[pltpu.VMEM((tm, d), BF16)],
        compiler_params=_params("parallel", "arbitrary"),
    )(x, norm_g, wg, wu, wd, post_g)
    return (res[0], tuple(res[1:])) if emit_bf16 else res[0]


def _norm_kernel(x_ref, g_ref, o_ref):
    o_ref[...] = _rms(x_ref[...], g_ref[...]).astype(o_ref.dtype)


def _norm(x, gain, *, tm=FFN_ROWS):
    n, d = x.shape
    tm = _tile(n, tm)
    return pl.pallas_call(
        _norm_kernel,
        grid=(n // tm,),
        in_specs=[pl.BlockSpec((tm, d), lambda i: (i, 0)), pl.BlockSpec((1, d), lambda i: (0, 0))],
        out_specs=pl.BlockSpec((tm, d), lambda i: (i, 0)),
        out_shape=jax.ShapeDtypeStruct((n, d), BF16),
        compiler_params=_params("parallel"),
    )(x, gain)


def _proj_kernel(h_ref, w_ref, *outs):
    acc = jnp.dot(h_ref[...], w_ref[...], preferred_element_type=F32)
    for o in outs:
        o[...] = acc.astype(o.dtype)


def _proj(h, w, col0, ncols, dtypes, *, tm=1024, tn=1024):
    n, d = h.shape
    tm = _tile(n, tm)
    tn = _tile(ncols, tn)
    if ncols == w.shape[1]:
        off = 0
    else:
        assert col0 % tn == 0
        off = col0 // tn
    return pl.pallas_call(
        _proj_kernel,
        grid=(n // tm, ncols // tn),
        in_specs=[
            pl.BlockSpec((tm, d), lambda i, j: (i, 0)),
            pl.BlockSpec((d, tn), lambda i, j: (0, off + j)),
        ],
        out_specs=[pl.BlockSpec((tm, tn), lambda i, j: (i, j)) for _ in dtypes],
        out_shape=[jax.ShapeDtypeStruct((n, ncols), dt) for dt in dtypes],
        compiler_params=_params("parallel", "arbitrary"),
    )(h, w)


def _qk_kernel(h_ref, w_ref, gain_ref, c_ref, s_ref, perm_ref, *outs, tn, sub):
    h = h_ref[...]
    gain = gain_ref[...]
    reps = sub // LANES
    cos = jnp.concatenate([c_ref[...]] * reps, axis=1)
    sin = jnp.concatenate([s_ref[...]] * reps, axis=1)
    perm = perm_ref[...]
    acc = jnp.dot(h, w_ref[...], preferred_element_type=F32)
    for s0 in range(0, tn, sub):
        xn = jnp.concatenate([_rms(acc[:, c0:c0 + LANES], gain) for c0 in range(s0, s0 + sub, LANES)], axis=1)
        hi = xn.astype(BF16)
        lo = (xn - hi.astype(F32)).astype(BF16)
        partner = jnp.dot(jnp.concatenate([hi, lo], axis=1), perm, preferred_element_type=F32)
        y = xn * cos + partner * sin
        for o in outs:
            o[:, s0:s0 + sub] = y.astype(o.dtype)


def _rope_perm(hd, rot, sub):
    half = rot // 2
    p = np.zeros((hd, hd), np.float32)
    for l in range(half):
        p[l + half, l] = -1.0
        p[l, l + half] = 1.0
    bd = np.kron(np.eye(sub // hd, dtype=np.float32), p)
    return np.concatenate([bd, bd], axis=0)


def _qk_proj(h, w, col0, ncols, gain, tables, dtypes, *, rot, tm=1024, tn=1024):
    n, d = h.shape
    tm = _tile(n, tm)
    tn = _tile(ncols, tn)
    cos, sin = tables
    hd = cos.shape[1]
    tm = _tile(cos.shape[0], tm)
    nt = cos.shape[0] // tm
    off = col0 // tn
    sub = min(tn, MXU_COLS)
    perm = jnp.asarray(_rope_perm(hd, rot, sub), BF16)
    tspec = pl.BlockSpec((tm, hd), lambda i, j: (i % nt, 0))
    return pl.pallas_call(
        functools.partial(_qk_kernel, tn=tn, sub=sub),
        grid=(n // tm, ncols // tn),
        in_specs=[
            pl.BlockSpec((tm, d), lambda i, j: (i, 0)),
            pl.BlockSpec((d, tn), lambda i, j: (0, off + j)),
            pl.BlockSpec((1, hd), lambda i, j: (0, 0)),
            tspec, tspec,
            pl.BlockSpec(perm.shape, lambda i, j: (0, 0)),
        ],
        out_specs=[pl.BlockSpec((tm, tn), lambda i, j: (i, j)) for _ in dtypes],
        out_shape=[jax.ShapeDtypeStruct((n, ncols), dt) for dt in dtypes],
        compiler_params=_params("parallel", "arbitrary"),
    )(h, w, gain, cos, sin, perm)


def _rope_tables(pos, hd, rot):
    half = rot // 2
    inv = ROPE_THETA ** (-jnp.arange(half, dtype=F32) * 2.0 / rot)
    ang = pos.astype(F32)[:, None] * inv[None, :]
    cos, sin = jnp.cos(ang), jnp.sin(ang)
    n = pos.shape[0]
    c = jnp.concatenate([cos, cos, jnp.ones((n, hd - rot), F32)], axis=1)
    s = jnp.concatenate([sin, sin, jnp.zeros((n, hd - rot), F32)], axis=1)
    return c, s


def _lambda(lam_ref, li):
    lv = lam_ref[...]
    a = jnp.sum(lv[0:1] * lv[1:2], axis=-1, keepdims=True)
    b = jnp.sum(lv[2:3] * lv[3:4], axis=-1, keepdims=True)
    return jnp.exp(a) - jnp.exp(b) + li


def _pattn_kernel(q_ref, k_ref, v_ref, lam_ref, gn_ref, o_ref, *, t, nq, hd, li):
    c1 = hd ** -0.5 * math.log2(math.e)
    lam = _lambda(lam_ref, li)
    row = lax.broadcasted_iota(jnp.int32, (t, t), 0)
    col = lax.broadcasted_iota(jnp.int32, (t, t), 1)

    for n in range(nq):
        past = n * t
        q = q_ref[past:past + t, :]
        k = k_ref[0:past + t, :]
        e, inv = [], []
        for c in range(2):
            s = lax.dot_general(q[:, c * hd:(c + 1) * hd], k[:, c * hd:(c + 1) * hd], NT_DIMS,
                                preferred_element_type=F32)
            diag = jnp.where(row >= col, s[:, past:], -jnp.inf)
            s = jnp.concatenate([s[:, :past], diag], axis=1) if past else diag
            ec = jnp.exp2(s * c1 - jnp.max(s, axis=-1, keepdims=True) * c1)
            e.append(ec)
            inv.append(1.0 / jnp.sum(ec, axis=-1, keepdims=True))
        w = (e[0] * inv[0] - e[1] * (lam * inv[1])).astype(BF16)
        o = jnp.dot(w, v_ref[0:past + t, :], preferred_element_type=F32)
        o_ref[past:past + t, :] = (_rms(o, gn_ref[...]) * (1.0 - li)).astype(o_ref.dtype)


def _prompt_attention(q, k, v, lam_vecs, gn, *, nb, seq, hd, li, t=256):
    n, width = q.shape
    heads = width // (2 * hd)
    vd = v.shape[1] // heads
    t = _tile(seq, t)
    return pl.pallas_call(
        functools.partial(_pattn_kernel, t=t, nq=seq // t, hd=hd, li=li),
        grid=(nb, heads),
        in_specs=[
            pl.BlockSpec((seq, 2 * hd), lambda b, h: (b, h)),
            pl.BlockSpec((seq, 2 * hd), lambda b, h: (b, h)),
            pl.BlockSpec((seq, vd), lambda b, h: (b, h)),
            pl.BlockSpec((4, hd), lambda b, h: (0, 0)),
            pl.BlockSpec((1, vd), lambda b, h: (0, 0)),
        ],
        out_specs=pl.BlockSpec((seq, vd), lambda b, h: (b, h)),
        out_shape=jax.ShapeDtypeStruct((n, heads * vd), BF16),
        compiler_params=_params("parallel", "parallel"),
    )(q, k, v, lam_vecs, gn)


def _sattn_kernel(pt_ref, q_ref, kn_ref, vn_ref, lam_ref, gn_ref, *rest, G, heads, hd, ld, li):
    kp = rest[:G]
    vp = rest[G:2 * G]
    o_ref, qm_ref, m_ref, l_ref, acc_ref = rest[2 * G:]
    p = pl.program_id(1)
    scale = hd ** -0.5
    w2 = 2 * hd
    vd = acc_ref.shape[1]
    rows = 2 * ld
    page = kp[0].shape[1]
    pcols = page * heads

    @pl.when(p == 0)
    def _():
        row = lax.broadcasted_iota(jnp.int32, (rows, w2), 0)
        lane = lax.broadcasted_iota(jnp.int32, (rows, w2), 1)
        own_map = (row < ld) == (lane < hd)
        qi = lax.broadcasted_iota(jnp.int32, (rows, 1), 0) % ld
        qm, m0, l0, a0 = [], [], [], []
        for h in range(heads):
            qh = q_ref[0][:, h * w2:(h + 1) * w2]
            qf = jnp.where(own_map, jnp.concatenate([qh, qh], axis=0), 0.0)
            kn = kn_ref[0][:, h * w2:(h + 1) * w2]
            vn = vn_ref[0][:, h * vd:(h + 1) * vd]
            s = [jnp.where(qi >= j, jnp.sum(qf * kn[j:j + 1, :], axis=-1, keepdims=True) * scale, -jnp.inf)
                 for j in range(ld)]
            m = functools.reduce(jnp.maximum, s)
            pj = [jnp.exp(sj - m) for sj in s]
            qm.append(qf)
            m0.append(m)
            l0.append(functools.reduce(jnp.add, pj))
            a0.append(functools.reduce(jnp.add, [pj[j] * vn[j:j + 1, :] for j in range(ld)]))
        qm_ref[...] = jnp.concatenate(qm, axis=0).astype(BF16)
        m_ref[...] = jnp.concatenate(m0, axis=0)
        l_ref[...] = jnp.concatenate(l0, axis=0)
        acc_ref[...] = jnp.concatenate(a0, axis=0)

    nr = heads * rows
    qmat = qm_ref[...]
    same_head = (lax.broadcasted_iota(jnp.int32, (nr, pcols), 0) // rows
                 == lax.broadcasted_iota(jnp.int32, (nr, pcols), 1) % heads)
    s = jnp.concatenate(
        [jnp.where(same_head,
                   lax.dot_general(qmat, kp[g][0].reshape(pcols, w2).astype(BF16), NT_DIMS,
                                   preferred_element_type=F32) * scale, -jnp.inf)
         for g in range(G)], axis=1)
    m_old = m_ref[...]
    m_new = jnp.maximum(m_old, jnp.max(s, axis=-1, keepdims=True))
    alpha = jnp.exp(m_old - m_new)
    pr = jnp.exp(s - m_new)
    pv = functools.reduce(jnp.add, [
        jnp.dot(pr[:, g * pcols:(g + 1) * pcols].astype(BF16), vp[g][0].reshape(pcols, vd).astype(BF16),
                preferred_element_type=F32) for g in range(G)])
    m_ref[...] = m_new
    l_ref[...] = alpha * l_ref[...] + jnp.sum(pr, axis=-1, keepdims=True)
    acc_ref[...] = alpha * acc_ref[...] + pv

    @pl.when(p == pl.num_programs(1) - 1)
    def _():
        lam = _lambda(lam_ref, li)
        w = acc_ref[...] / l_ref[...]
        for h in range(heads):
            o = w[h * rows:h * rows + ld] - lam * w[h * rows + ld:(h + 1) * rows]
            o_ref[0, :, h * vd:(h + 1) * vd] = _rms(o, gn_ref[...]) * (1.0 - li)


def _sample_attention(q, kn, vn, cache_k, cache_v, page_table, lam_vecs, gn, *, pool0, hd, li, G=8):
    nb, ld, width = q.shape
    _, page, heads, w2 = cache_k.shape
    vwidth = vn.shape[2]
    vd = vwidth // heads
    n_pages = page_table.shape[1]
    G = _tile(n_pages, G)

    def page_spec(w, g):
        return pl.BlockSpec((1, page, heads, w), lambda b, p, pt: (pool0 + pt[b, p * G + g], 0, 0, 0))

    grid_spec = pltpu.PrefetchScalarGridSpec(
        num_scalar_prefetch=1,
        grid=(nb, n_pages // G),
        in_specs=[
            pl.BlockSpec((1, ld, width), lambda b, p, pt: (b, 0, 0)),
            pl.BlockSpec((1, ld, width), lambda b, p, pt: (b, 0, 0)),
            pl.BlockSpec((1, ld, vwidth), lambda b, p, pt: (b, 0, 0)),
            pl.BlockSpec((4, hd), lambda b, p, pt: (0, 0)),
            pl.BlockSpec((1, vd), lambda b, p, pt: (0, 0)),
        ] + [page_spec(w2, g) for g in range(G)] + [page_spec(vd, g) for g in range(G)],
        out_specs=pl.BlockSpec((1, ld, vwidth), lambda b, p, pt: (b, 0, 0)),
        scratch_shapes=[
            pltpu.VMEM((heads * 2 * ld, w2), BF16),
            pltpu.VMEM((heads * 2 * ld, 1), F32),
            pltpu.VMEM((heads * 2 * ld, 1), F32),
            pltpu.VMEM((heads * 2 * ld, vd), F32),
        ],
    )
    return pl.pallas_call(
        functools.partial(_sattn_kernel, G=G, heads=heads, hd=hd, ld=ld, li=li),
        grid_spec=grid_spec,
        out_shape=jax.ShapeDtypeStruct((nb, ld, vwidth), F32),
        compiler_params=_params("parallel", "arbitrary"),
    )(page_table, q, kn, vn, lam_vecs, gn, *([cache_k] * G), *([cache_v] * G))


def _gla_tables(c):
    t = np.arange(c)[:, None]
    u = np.arange(c)[None, :]
    mats = [u <= t, u > t]
    masks = []
    m = c // 2
    while m >= 1:
        mats.append((u > (t // m) * m) & (u <= t))
        mats.append((u > t) & (u <= (t // m + 1) * m))
        masks.append(((t // m) % 2 == 1) & ((u // m) == (t // m) - 1))
        m //= 2
    masks.append(t == u)
    return (np.concatenate(mats, 0).astype(np.float32), np.stack(masks, 0).astype(np.float32))


def _gla_kernel(q_ref, k_ref, v_ref, gd_ref, r_ref, s0_ref, wg_ref, bg_ref, gn_ref, mat_ref, mask_ref,
                o_ref, sout_ref, s_ref, *, c, valid, qscale):
    n = pl.program_id(1)
    heads, dk, dv = s_ref.shape

    @pl.when(n == 0)
    def _():
        s_ref[...] = s0_ref[0]

    mats = mat_ref[...]
    nlev = mask_ref.shape[0] - 1
    hs = range(heads)
    ks = [slice(h * dk, (h + 1) * dk) for h in hs]
    vs = [slice(h * dv, (h + 1) * dv) for h in hs]

    x = jnp.dot(gd_ref[...].astype(BF16), wg_ref[...], preferred_element_type=F32) + bg_ref[...]
    logg = (jnp.minimum(x, 0.0) - jnp.log1p(jnp.exp(-jnp.abs(x)))) * (1.0 / GATE_TAU)
    if valid < c:
        logg = jnp.where(lax.broadcasted_iota(jnp.int32, logg.shape, 0) < valid, logg, 0.0)
    hi = logg.astype(BF16)
    lo = (logg - hi.astype(F32)).astype(BF16)
    e = jnp.exp(jnp.dot(mats, hi, preferred_element_type=F32)
                + jnp.dot(mats, lo, preferred_element_type=F32))

    q = q_ref[...] * qscale
    k = k_ref[...]
    v = v_ref[...].astype(BF16)
    qb = q.astype(BF16)
    kb = k.astype(BF16)
    q_in = (q * e[0:c]).astype(BF16)
    k_end = (k * e[c:2 * c]).astype(BF16)
    s_old = [s_ref[h] for h in hs]

    o = [jnp.dot(q_in[:, ks[h]], s_old[h].astype(BF16), preferred_element_type=F32) for h in hs]
    a = [mask_ref[nlev] * lax.dot_general(qb[:, ks[h]], kb[:, ks[h]], NT_DIMS, preferred_element_type=F32)
         for h in hs]
    for lv in range(nlev):
        qt = (q * e[(2 + 2 * lv) * c:(3 + 2 * lv) * c]).astype(BF16)
        kt = (k * e[(3 + 2 * lv) * c:(4 + 2 * lv) * c]).astype(BF16)
        a = [a[h] + mask_ref[lv] * lax.dot_general(qt[:, ks[h]], kt[:, ks[h]], NT_DIMS,
                                                   preferred_element_type=F32) for h in hs]
    o = [o[h] + jnp.dot(a[h].astype(BF16), v[:, vs[h]], preferred_element_type=F32) for h in hs]
    upd = [lax.dot_general(k_end[:, ks[h]], v[:, vs[h]], TN_DIMS, preferred_element_type=F32) for h in hs]

    decay_cols = jnp.broadcast_to(e[c - 1:c], (LANES, heads * dk)).T
    for h in hs:
        decay = jnp.concatenate([decay_cols[ks[h]]] * (dv // LANES), axis=1)
        s_ref[h] = decay * s_old[h] + upd[h]
        r = r_ref[:, vs[h]]
        o_ref[:, vs[h]] = (_rms(o[h], gn_ref[...]) * (r * jax.nn.sigmoid(r))).astype(o_ref.dtype)

    @pl.when(n == pl.num_programs(1) - 1)
    def _():
        sout_ref[0] = s_ref[...]


def _gla(z, zcols, gd, s0, batch0, wg, bg, gn, *, nb, seq, c, valid, out_dtype):
    _, heads, dk, dv = s0.shape
    rank = gd.shape[1]
    nc = seq // c
    wk, wv = heads * dk, heads * dv
    qc, kc, vc, rc = zcols
    assert qc % wk == 0 and kc % wk == 0 and vc % wv == 0 and rc % wv == 0
    mats, masks = _gla_tables(c)
    mats = jnp.asarray(mats, BF16)
    masks = jnp.asarray(masks, F32)

    def zspec(w, col0):
        return pl.BlockSpec((c, w), lambda b, n: (b * nc + n, col0 // w))

    return pl.pallas_call(
        functools.partial(_gla_kernel, c=c, valid=valid, qscale=dk ** -0.5),
        grid=(nb, nc),
        in_specs=[
            zspec(wk, qc), zspec(wk, kc), zspec(wv, vc),
            pl.BlockSpec((c, rank), lambda b, n: (b * nc + n, 0)),
            zspec(wv, rc),
            pl.BlockSpec((1, heads, dk, dv), lambda b, n: (batch0 + b, 0, 0, 0)),
            pl.BlockSpec((rank, wk), lambda b, n: (0, 0)),
            pl.BlockSpec((1, wk), lambda b, n: (0, 0)),
            pl.BlockSpec((1, dv), lambda b, n: (0, 0)),
            pl.BlockSpec(mats.shape, lambda b, n: (0, 0)),
            pl.BlockSpec(masks.shape, lambda b, n: (0, 0, 0)),
        ],
        out_specs=[
            pl.BlockSpec((c, wv), lambda b, n: (b * nc + n, 0)),
            pl.BlockSpec((1, heads, dk, dv), lambda b, n: (b, 0, 0, 0)),
        ],
        out_shape=[
            jax.ShapeDtypeStruct((nb * seq, wv), out_dtype),
            jax.ShapeDtypeStruct((nb, heads, dk, dv), F32),
        ],
        scratch_shapes=[pltpu.VMEM((heads, dk, dv), F32)],
        compiler_params=_params("parallel", "arbitrary"),
    )(z, z, z, gd, z, s0, wg, bg, gn, mats, masks)


def _oproj_kernel(oa_ref, og_ref, wa_ref, wg_ref, x_ref, y_ref):
    acc = jnp.dot(oa_ref[...], wa_ref[...], preferred_element_type=F32)
    acc = acc + jnp.dot(og_ref[...], wg_ref[...], preferred_element_type=F32)
    y_ref[...] = x_ref[...] + acc


def _oproj(oa, og, w, x, *, tm=1024, tn=1024):
    n, d = x.shape
    wa_rows = oa.shape[1]
    wg_rows = og.shape[1]
    assert wa_rows == wg_rows
    tm = _tile(n, tm)
    tn = _tile(d, tn)
    return pl.pallas_call(
        _oproj_kernel,
        grid=(n // tm, d // tn),
        in_specs=[
            pl.BlockSpec((tm, wa_rows), lambda i, j: (i, 0)),
            pl.BlockSpec((tm, wg_rows), lambda i, j: (i, 0)),
            pl.BlockSpec((wa_rows, tn), lambda i, j: (0, j)),
            pl.BlockSpec((wg_rows, tn), lambda i, j: (1, j)),
            pl.BlockSpec((tm, tn), lambda i, j: (i, j)),
        ],
        out_specs=pl.BlockSpec((tm, tn), lambda i, j: (i, j)),
        out_shape=jax.ShapeDtypeStruct((n, d), F32),
        compiler_params=_params("parallel", "arbitrary"),
    )(oa, og, w, w, x)


def _row(v):
    return v.reshape(1, -1).astype(F32)


def kernel(x_prompt, x_sample, cache_k, cache_v, state_gla, page_table, w_in, attn_q_norm, attn_k_norm, lam_q1, lam_k1, lam_q2, lam_k2, attn_out_norm, gla_w_gate_up, gla_b_gate, gla_out_norm, w_out, ffn1_norm, ffn1_w_gate, ffn1_w_up, ffn1_w_down, mix_norm, ffn2_norm, ffn2_w_gate, ffn2_w_up, ffn2_w_down, final_norm):
    nbp, seq, d = x_prompt.shape
    nbs, ld, _ = x_sample.shape
    depth, n_pool, page, heads_a, w2 = cache_k.shape
    hd = w2 // 2
    vd = cache_v.shape[-1]
    _, _, heads_g, dk, dv = state_gla.shape
    rank = gla_w_gate_up.shape[1]
    past = page_table.shape[1] * page
    rot = hd // 4
    wq = heads_a * w2
    wv = heads_a * vd
    wgq = heads_g * dk
    wgv = heads_g * dv
    c_q, c_k, c_v = 0, wq, 2 * wq
    c_rest = 2 * wq + wv
    w_rest = 2 * wgq + 2 * wgv
    c_gd = c_rest + w_rest
    zcols = (0, wgq, 2 * wgq, 2 * wgq + wgv)

    tab_p = _rope_tables(jnp.arange(seq), hd, rot)
    tab_s = _rope_tables(past + jnp.arange(nbs * ld) % ld, hd, rot)
    chunk = math.gcd(seq, GLA_CHUNK)
    ld_pad = -(-ld // 8) * 8

    xp = x_prompt.reshape(nbp * seq, d)
    xs = x_sample.reshape(nbs * ld, d)
    cache_k4 = cache_k.reshape(depth * n_pool, page, heads_a, w2)
    cache_v4 = cache_v.reshape(depth * n_pool, page, heads_a, vd)
    state4 = state_gla.reshape(depth * nbs, heads_g, dk, dv)
    zero_state = jnp.zeros((nbp, heads_g, dk, dv), F32)
    outs = [[] for _ in range(6)]
    for l in range(depth):
        li = 0.8 - 0.6 * math.exp(-0.3 * l)
        ffn1_w = (ffn1_w_gate[l], ffn1_w_up[l], ffn1_w_down[l])
        ffn2_w = (ffn2_w_gate[l], ffn2_w_up[l], ffn2_w_down[l])
        sample_casts = nbs * ld <= FFN_ROWS
        if not sample_casts:
            ffn1_w = tuple(w.astype(BF16) for w in ffn1_w)
            ffn2_w = tuple(w.astype(BF16) for w in ffn2_w)
        w_in_b = w_in[l].astype(BF16)
        w_gd = w_in_b[:, c_gd:]
        w_out_b = w_out[l].astype(BF16)
        lam_vecs = jnp.stack([lam_q1[l], lam_k1[l], lam_q2[l], lam_k2[l]]).astype(F32)
        gq, gk, gn_a = _row(attn_q_norm[l]), _row(attn_k_norm[l]), _row(attn_out_norm[l])
        wg = gla_w_gate_up[l].astype(BF16)
        bg, gn_g = _row(gla_b_gate[l]), _row(gla_out_norm[l])

        def ffn(x, norm, w, post, final, emit):
            res = _ffn(x, _row(norm), *w, _row(post), final=final, emit_bf16=emit)
            return res if emit else (res, w)

        def dense_in(x, tables, q_dtype, w, emit):
            x1, w = ffn(x, ffn1_norm[l], w, ffn1_norm[l], False, emit)
            h = _norm(x1, _row(mix_norm[l]))
            (qa,) = _qk_proj(h, w_in_b, c_q, wq, gq, tables, [q_dtype], rot=rot)
            ka, ka_b = _qk_proj(h, w_in_b, c_k, wq, gk, tables, [F32, BF16], rot=rot)
            va, va_b = _proj(h, w_in_b, c_v, wv, [F32, BF16])
            (z,) = _proj(h, w_in_b, c_rest, w_rest, [F32])
            (gd,) = _proj(h, w_gd, 0, rank, [F32])
            return (x1, qa, ka, ka_b, va, va_b, z, gd), w

        def dense_out(x1, oa, og, w, emit):
            x2 = _oproj(oa, og, w_out_b, x1)
            return ffn(x2, ffn2_norm[l], w, final_norm[l], True, emit)

        (x1, qa, ka, ka_b, va, va_b, z, gd), ffn1_w = dense_in(xs, tab_s, F32, ffn1_w, sample_casts)
        oa = _sample_attention(qa.reshape(nbs, ld, wq), ka.reshape(nbs, ld, wq), va.reshape(nbs, ld, wv),
                               cache_k4, cache_v4, page_table, lam_vecs, gn_a, pool0=l * n_pool, hd=hd, li=li)
        pad = lambda a: jnp.pad(a.reshape(nbs, ld, -1), ((0, 0), (0, ld_pad - ld), (0, 0))).reshape(nbs * ld_pad, -1)
        og, s_s = _gla(pad(z), zcols, pad(gd), state4, l * nbs, wg, bg, gn_g,
                       nb=nbs, seq=ld_pad, c=ld_pad, valid=ld, out_dtype=F32)
        og = og.reshape(nbs, ld_pad, wgv)[:, :ld].reshape(nbs * ld, wgv)
        xs, ffn2_w = dense_out(x1, oa.reshape(nbs * ld, wv).astype(BF16), og.astype(BF16), ffn2_w, sample_casts)
        outs[3].append(ka.reshape(nbs, ld, heads_a, w2))
        outs[4].append(va.reshape(nbs, ld, heads_a, vd))
        outs[5].append(s_s)

        (x1, qa, ka, ka_b, va, va_b, z, gd), _ = dense_in(xp, tab_p, BF16, ffn1_w, False)
        oa = _prompt_attention(qa, ka_b, va_b, lam_vecs, gn_a, nb=nbp, seq=seq, hd=hd, li=li)
        og, s_p = _gla(z, zcols, gd, zero_state, 0, wg, bg, gn_g,
                       nb=nbp, seq=seq, c=chunk, valid=chunk, out_dtype=BF16)
        xp, _ = dense_out(x1, oa, og, ffn2_w, False)
        outs[0].append(ka.reshape(nbp, seq, heads_a, w2))
        outs[1].append(va.reshape(nbp, seq, heads_a, vd))
        outs[2].append(s_p)

    return (xp.reshape(nbp, seq, d), xs.reshape(nbs, ld, d), jnp.stack(outs[0]), jnp.stack(outs[1]),
            jnp.stack(outs[2]), jnp.stack(outs[3]), jnp.stack(outs[4]), jnp.stack(outs[5]))
```

```python
import functools
import math

import numpy as np
import jax
import jax.numpy as jnp
from jax import lax
from jax.experimental import pallas as pl
from jax.experimental.pallas import tpu as pltpu

F32 = jnp.float32
BF16 = jnp.bfloat16
EPS = 1e-6
ROPE_THETA = 500000.0
GATE_TAU = 16.0
FFN_ROWS = 512
GLA_CHUNK = 128
LANES = 128
SUBLANES = 8
MXU_COLS = 256
VMEM_LIMIT = 60 * 1024 * 1024
NT_DIMS = (((1,), (1,)), ((), ()))
TN_DIMS = (((0,), (0,)), ((), ()))


def _params(*sem):
    return pltpu.CompilerParams(dimension_semantics=sem, vmem_limit_bytes=VMEM_LIMIT)


def _rms(x, gain):
    ms = jnp.mean(x * x, axis=-1, keepdims=True)
    return x * lax.rsqrt(ms + EPS) * gain


def _tile(n, want):
    t = min(n, want)
    while n % t:
        t //= 2
    return t


def _ffn_kernel(x_ref, g_ref, wg_ref, wu_ref, wd_ref, pg_ref, *rest, final, emit_bf16, nf, tail):
    rest = list(rest)
    y_ref = rest.pop(0)
    wb_refs = [rest.pop(0) for _ in range(3)] if emit_bf16 else None
    (xn_ref,) = rest
    j = pl.program_id(1)
    last = nf - 1
    tf = wd_ref.shape[0]

    @pl.when(j == 0)
    def _():
        xn_ref[...] = _rms(x_ref[...], g_ref[...]).astype(BF16)

    def accumulate(width, first):
        if emit_bf16:
            for ref, w_ref in zip(wb_refs, (wg_ref, wu_ref, wd_ref)):
                ref[...] = w_ref[...].astype(BF16)
            wg, wu, wd = (ref[...] for ref in wb_refs)
        else:
            wg, wu, wd = wg_ref[:, :width], wu_ref[:, :width], wd_ref[:width, :]
        xn = xn_ref[...]
        g = jnp.dot(xn, wg, preferred_element_type=F32)
        u = jnp.dot(xn, wu, preferred_element_type=F32)
        h = (g * jax.nn.sigmoid(g) * u).astype(BF16)
        acc = jnp.dot(h, wd, preferred_element_type=F32)
        if first:
            y_ref[...] = acc
        else:
            y_ref[...] += acc

    if nf == 1:
        accumulate(tail, True)
    else:
        pl.when(j == 0)(functools.partial(accumulate, tf, True))
        if tail == tf:
            pl.when(j > 0)(functools.partial(accumulate, tf, False))
        else:
            pl.when((j > 0) & (j < last))(functools.partial(accumulate, tf, False))
            pl.when(j == last)(functools.partial(accumulate, tail, False))

    @pl.when(j == last)
    def _():
        y = x_ref[...] + 0.5 * y_ref[...]
        y_ref[...] = _rms(y, pg_ref[...]) if final else y


def _ffn(x, norm_g, wg, wu, wd, post_g, *, final, emit_bf16, tm=FFN_ROWS, tf=512):
    n, d = x.shape
    dff = wd.shape[0]
    single = {}
    if emit_bf16:
        tf = _tile(dff, tf // 2)
        assert n <= tm
        single = dict(pipeline_mode=pl.Buffered(1))
    nf = pl.cdiv(dff, tf)
    tail = dff - (nf - 1) * tf
    tm = _tile(n, tm)
    out_shape = [jax.ShapeDtypeStruct((n, d), F32)]
    out_specs = [pl.BlockSpec((tm, d), lambda i, j: (i, 0), **single)]
    if emit_bf16:
        out_shape += [jax.ShapeDtypeStruct(w.shape, BF16) for w in (wg, wu, wd)]
        out_specs += [pl.BlockSpec((d, tf), lambda i, j: (0, j), **single),
                      pl.BlockSpec((d, tf), lambda i, j: (0, j), **single),
                      pl.BlockSpec((tf, d), lambda i, j: (j, 0), **single)]
    res = pl.pallas_call(
        functools.partial(_ffn_kernel, final=final, emit_bf16=emit_bf16, nf=nf, tail=tail),
        grid=(n // tm, nf),
        in_specs=[
            pl.BlockSpec((tm, d), lambda i, j: (i, 0), pipeline_mode=pl.Buffered(1)),
            pl.BlockSpec((1, d), lambda i, j: (0, 0)),
            pl.BlockSpec((d, tf), lambda i, j: (0, j)),
            pl.BlockSpec((d, tf), lambda i, j: (0, j)),
            pl.BlockSpec((tf, d), lambda i, j: (j, 0)),
            pl.BlockSpec((1, d), lambda i, j: (0, 0)),
        ],
        out_specs=out_specs,
        out_shape=out_shape,
        scratch_shapes=[pltpu.VMEM((tm, d), BF16)],
        compiler_params=_params("parallel", "arbitrary"),
    )(x, norm_g, wg, wu, wd, post_g)
    return (res[0], tuple(res[1:])) if emit_bf16 else res[0]


def _norm_kernel(x_ref, g_ref, o_ref):
    o_ref[...] = _rms(x_ref[...], g_ref[...]).astype(o_ref.dtype)


def _norm(x, gain, *, tm=FFN_ROWS):
    n, d = x.shape
    tm = _tile(n, tm)
    return pl.pallas_call(
        _norm_kernel,
        grid=(n // tm,),
        in_specs=[pl.BlockSpec((tm, d), lambda i: (i, 0)), pl.BlockSpec((1, d), lambda i: (0, 0))],
        out_specs=pl.BlockSpec((tm, d), lambda i: (i, 0)),
        out_shape=jax.ShapeDtypeStruct((n, d), BF16),
        compiler_params=_params("parallel"),
    )(x, gain)


def _proj_kernel(h_ref, w_ref, *outs):
    acc = jnp.dot(h_ref[...], w_ref[...], preferred_element_type=F32)
    for o in outs:
        o[...] = acc.astype(o.dtype)


def _proj(h, w, col0, ncols, dtypes, *, tm=1024, tn=1024):
    n, d = h.shape
    tm = _tile(n, tm)
    tn = _tile(ncols, tn)
    if ncols == w.shape[1]:
        off = 0
    else:
        assert col0 % tn == 0
        off = col0 // tn
    return pl.pallas_call(
        _proj_kernel,
        grid=(n // tm, ncols // tn),
        in_specs=[
            pl.BlockSpec((tm, d), lambda i, j: (i, 0)),
            pl.BlockSpec((d, tn), lambda i, j: (0, off + j)),
        ],
        out_specs=[pl.BlockSpec((tm, tn), lambda i, j: (i, j)) for _ in dtypes],
        out_shape=[jax.ShapeDtypeStruct((n, ncols), dt) for dt in dtypes],
        compiler_params=_params("parallel", "arbitrary"),
    )(h, w)


def _qk_kernel(h_ref, w_ref, gain_ref, c_ref, s_ref, perm_ref, *outs, tn, sub):
    h = h_ref[...]
    gain = gain_ref[...]
    reps = sub // LANES
    cos = jnp.concatenate([c_ref[...]] * reps, axis=1)
    sin = jnp.concatenate([s_ref[...]] * reps, axis=1)
    perm = perm_ref[...]
    acc = jnp.dot(h, w_ref[...], preferred_element_type=F32)
    for s0 in range(0, tn, sub):
        xn = jnp.concatenate([_rms(acc[:, c0:c0 + LANES], gain) for c0 in range(s0, s0 + sub, LANES)], axis=1)
        hi = xn.astype(BF16)
        lo = (xn - hi.astype(F32)).astype(BF16)
        partner = jnp.dot(jnp.concatenate([hi, lo], axis=1), perm, preferred_element_type=F32)
        y = xn * cos + partner * sin
        for o in outs:
            o[:, s0:s0 + sub] = y.astype(o.dtype)


def _rope_perm(hd, rot, sub):
    half = rot // 2
    p = np.zeros((hd, hd), np.float32)
    for l in range(half):
        p[l + half, l] = -1.0
        p[l, l + half] = 1.0
    bd = np.kron(np.eye(sub // hd, dtype=np.float32), p)
    return np.concatenate([bd, bd], axis=0)


def _qk_proj(h, w, col0, ncols, gain, tables, dtypes, *, rot, tm=1024, tn=1024):
    n, d = h.shape
    tm = _tile(n, tm)
    tn = _tile(ncols, tn)
    cos, sin = tables
    hd = cos.shape[1]
    tm = _tile(cos.shape[0], tm)
    nt = cos.shape[0] // tm
    off = col0 // tn
    sub = min(tn, MXU_COLS)
    perm = jnp.asarray(_rope_perm(hd, rot, sub), BF16)
    tspec = pl.BlockSpec((tm, hd), lambda i, j: (i % nt, 0))
    return pl.pallas_call(
        functools.partial(_qk_kernel, tn=tn, sub=sub),
        grid=(n // tm, ncols // tn),
        in_specs=[
            pl.BlockSpec((tm, d), lambda i, j: (i, 0)),
            pl.BlockSpec((d, tn), lambda i, j: (0, off + j)),
            pl.BlockSpec((1, hd), lambda i, j: (0, 0)),
            tspec, tspec,
            pl.BlockSpec(perm.shape, lambda i, j: (0, 0)),
        ],
        out_specs=[pl.BlockSpec((tm, tn), lambda i, j: (i, j)) for _ in dtypes],
        out_shape=[jax.ShapeDtypeStruct((n, ncols), dt) for dt in dtypes],
        compiler_params=_params("parallel", "arbitrary"),
    )(h, w, gain, cos, sin, perm)


def _rope_tables(pos, hd, rot):
    half = rot // 2
    inv = ROPE_THETA ** (-jnp.arange(half, dtype=F32) * 2.0 / rot)
    ang = pos.astype(F32)[:, None] * inv[None, :]
    cos, sin = jnp.cos(ang), jnp.sin(ang)
    n = pos.shape[0]
    c = jnp.concatenate([cos, cos, jnp.ones((n, hd - rot), F32)], axis=1)
    s = jnp.concatenate([sin, sin, jnp.zeros((n, hd - rot), F32)], axis=1)
    return c, s


def _lambda(lam_ref, li):
    lv = lam_ref[...]
    a = jnp.sum(lv[0:1] * lv[1:2], axis=-1, keepdims=True)
    b = jnp.sum(lv[2:3] * lv[3:4], axis=-1, keepdims=True)
    return jnp.exp(a) - jnp.exp(b) + li


def _pattn_kernel(q_ref, k_ref, v_ref, lam_ref, gn_ref, o_ref, *, t, nq, hd, li):
    c1 = hd ** -0.5 * math.log2(math.e)
    lam = _lambda(lam_ref, li)
    row = lax.broadcasted_iota(jnp.int32, (t, t), 0)
    col = lax.broadcasted_iota(jnp.int32, (t, t), 1)

    for n in range(nq):
        past = n * t
        q = q_ref[past:past + t, :]
        k = k_ref[0:past + t, :]
        e, inv = [], []
        for c in range(2):
            s = lax.dot_general(q[:, c * hd:(c + 1) * hd], k[:, c * hd:(c + 1) * hd], NT_DIMS,
                                preferred_element_type=F32)
            diag = jnp.where(row >= col, s[:, past:], -jnp.inf)
            s = jnp.concatenate([s[:, :past], diag], axis=1) if past else diag
            ec = jnp.exp2(s * c1 - jnp.max(s, axis=-1, keepdims=True) * c1)
            e.append(ec)
            inv.append(1.0 / jnp.sum(ec, axis=-1, keepdims=True))
        w = (e[0] * inv[0] - e[1] * (lam * inv[1])).astype(BF16)
        o = jnp.dot(w, v_ref[0:past + t, :], preferred_element_type=F32)
        o_ref[past:past + t, :] = (_rms(o, gn_ref[...]) * (1.0 - li)).astype(o_ref.dtype)


def _prompt_attention(q, k, v, lam_vecs, gn, *, nb, seq, hd, li, t=256):
    n, width = q.shape
    heads = width // (2 * hd)
    vd = v.shape[1] // heads
    t = _tile(seq, t)
    return pl.pallas_call(
        functools.partial(_pattn_kernel, t=t, nq=seq // t, hd=hd, li=li),
        grid=(nb, heads),
        in_specs=[
            pl.BlockSpec((seq, 2 * hd), lambda b, h: (b, h)),
            pl.BlockSpec((seq, 2 * hd), lambda b, h: (b, h)),
            pl.BlockSpec((seq, vd), lambda b, h: (b, h)),
            pl.BlockSpec((4, hd), lambda b, h: (0, 0)),
            pl.BlockSpec((1, vd), lambda b, h: (0, 0)),
        ],
        out_specs=pl.BlockSpec((seq, vd), lambda b, h: (b, h)),
        out_shape=jax.ShapeDtypeStruct((n, heads * vd), BF16),
        compiler_params=_params("parallel", "parallel"),
    )(q, k, v, lam_vecs, gn)


def _sattn_kernel(pt_ref, q_ref, kn_ref, vn_ref, lam_ref, gn_ref, *rest, G, heads, hd, ld, li):
    kp = rest[:G]
    vp = rest[G:2 * G]
    o_ref, qm_ref, m_ref, l_ref, acc_ref = rest[2 * G:]
    p = pl.program_id(1)
    scale = hd ** -0.5
    w2 = 2 * hd
    vd = acc_ref.shape[1]
    rows = 2 * ld
    page = kp[0].shape[1]
    pcols = page * heads

    @pl.when(p == 0)
    def _():
        row = lax.broadcasted_iota(jnp.int32, (rows, w2), 0)
        lane = lax.broadcasted_iota(jnp.int32, (rows, w2), 1)
        own_map = (row < ld) == (lane < hd)
        qi = lax.broadcasted_iota(jnp.int32, (rows, 1), 0) % ld
        qm, m0, l0, a0 = [], [], [], []
        for h in range(heads):
            qh = q_ref[0][:, h * w2:(h + 1) * w2]
            qf = jnp.where(own_map, jnp.concatenate([qh, qh], axis=0), 0.0)
            kn = kn_ref[0][:, h * w2:(h + 1) * w2]
            vn = vn_ref[0][:, h * vd:(h + 1) * vd]
            s = [jnp.where(qi >= j, jnp.sum(qf * kn[j:j + 1, :], axis=-1, keepdims=True) * scale, -jnp.inf)
                 for j in range(ld)]
            m = functools.reduce(jnp.maximum, s)
            pj = [jnp.exp(sj - m) for sj in s]
            qm.append(qf)
            m0.append(m)
            l0.append(functools.reduce(jnp.add, pj))
            a0.append(functools.reduce(jnp.add, [pj[j] * vn[j:j + 1, :] for j in range(ld)]))
        qm_ref[...] = jnp.concatenate(qm, axis=0).astype(BF16)
        m_ref[...] = jnp.concatenate(m0, axis=0)
        l_ref[...] = jnp.concatenate(l0, axis=0)
        acc_ref[...] = jnp.concatenate(a0, axis=0)

    nr = heads * rows
    qmat = qm_ref[...]
    same_head = (lax.broadcasted_iota(jnp.int32, (nr, pcols), 0) // rows
                 == lax.broadcasted_iota(jnp.int32, (nr, pcols), 1) % heads)
    s = jnp.concatenate(
        [jnp.where(same_head,
                   lax.dot_general(qmat, kp[g][0].reshape(pcols, w2).astype(BF16), NT_DIMS,
                                   preferred_element_type=F32) * scale, -jnp.inf)
         for g in range(G)], axis=1)
    m_old = m_ref[...]
    m_new = jnp.maximum(m_old, jnp.max(s, axis=-1, keepdims=True))
    alpha = jnp.exp(m_old - m_new)
    pr = jnp.exp(s - m_new)
    pv = functools.reduce(jnp.add, [
        jnp.dot(pr[:, g * pcols:(g + 1) * pcols].astype(BF16), vp[g][0].reshape(pcols, vd).astype(BF16),
                preferred_element_type=F32) for g in range(G)])
    m_ref[...] = m_new
    l_ref[...] = alpha * l_ref[...] + jnp.sum(pr, axis=-1, keepdims=True)
    acc_ref[...] = alpha * acc_ref[...] + pv

    @pl.when(p == pl.num_programs(1) - 1)
    def _():
        lam = _lambda(lam_ref, li)
        w = acc_ref[...] / l_ref[...]
        for h in range(heads):
            o = w[h * rows:h * rows + ld] - lam * w[h * rows + ld:(h + 1) * rows]
            o_ref[0, :, h * vd:(h + 1) * vd] = _rms(o, gn_ref[...]) * (1.0 - li)


def _sample_attention(q, kn, vn, cache_k, cache_v, page_table, lam_vecs, gn, *, pool0, hd, li, G=8):
    nb, ld, width = q.shape
    _, page, heads, w2 = cache_k.shape
    vwidth = vn.shape[2]
    vd = vwidth // heads
    n_pages = page_table.shape[1]
    G = _tile(n_pages, G)

    def page_spec(w, g):
        return pl.BlockSpec((1, page, heads, w), lambda b, p, pt: (pool0 + pt[b, p * G + g], 0, 0, 0))

    grid_spec = pltpu.PrefetchScalarGridSpec(
        num_scalar_prefetch=1,
        grid=(nb, n_pages // G),
        in_specs=[
            pl.BlockSpec((1, ld, width), lambda b, p, pt: (b, 0, 0)),
            pl.BlockSpec((1, ld, width), lambda b, p, pt: (b, 0, 0)),
            pl.BlockSpec((1, ld, vwidth), lambda b, p, pt: (b, 0, 0)),
            pl.BlockSpec((4, hd), lambda b, p, pt: (0, 0)),
            pl.BlockSpec((1, vd), lambda b, p, pt: (0, 0)),
        ] + [page_spec(w2, g) for g in range(G)] + [page_spec(vd, g) for g in range(G)],
        out_specs=pl.BlockSpec((1, ld, vwidth), lambda b, p, pt: (b, 0, 0)),
        scratch_shapes=[
            pltpu.VMEM((heads * 2 * ld, w2), BF16),
            pltpu.VMEM((heads * 2 * ld, 1), F32),
            pltpu.VMEM((heads * 2 * ld, 1), F32),
            pltpu.VMEM((heads * 2 * ld, vd), F32),
        ],
    )
    return pl.pallas_call(
        functools.partial(_sattn_kernel, G=G, heads=heads, hd=hd, ld=ld, li=li),
        grid_spec=grid_spec,
        out_shape=jax.ShapeDtypeStruct((nb, ld, vwidth), F32),
        compiler_params=_params("parallel", "arbitrary"),
    )(page_table, q, kn, vn, lam_vecs, gn, *([cache_k] * G), *([cache_v] * G))


def _gla_tables(c):
    t = np.arange(c)[:, None]
    u = np.arange(c)[None, :]
    mats = [u <= t]
    masks = []
    m = c // 2
    while m >= 1:
        if m < SUBLANES:
            mats.append((u > (t // m) * m) & (u <= t))
            mats.append((u > t) & (u <= (t // m + 1) * m))
        masks.append(((t // m) % 2 == 1) & ((u // m) == (t // m) - 1))
        m //= 2
    masks.append(t == u)
    return (np.concatenate(mats, 0).astype(np.float32), np.stack(masks, 0).astype(np.float32))


def _gla_kernel(q_ref, k_ref, v_ref, gd_ref, r_ref, s0_ref, wg_ref, bg_ref, gn_ref, mat_ref, mask_ref,
                o_ref, sout_ref, s_ref, *, c, valid, qscale):
    n = pl.program_id(1)
    heads, dk, dv = s_ref.shape

    @pl.when(n == 0)
    def _():
        s_ref[...] = s0_ref[0]

    mats = mat_ref[...]
    nlev = mask_ref.shape[0] - 1
    hs = range(heads)
    ks = [slice(h * dk, (h + 1) * dk) for h in hs]
    vs = [slice(h * dv, (h + 1) * dv) for h in hs]

    x = jnp.dot(gd_ref[...].astype(BF16), wg_ref[...], preferred_element_type=F32) + bg_ref[...]
    logg = (jnp.minimum(x, 0.0) - jnp.log1p(jnp.exp(-jnp.abs(x)))) * (1.0 / GATE_TAU)
    if valid < c:
        logg = jnp.where(lax.broadcasted_iota(jnp.int32, logg.shape, 0) < valid, logg, 0.0)
    hi = logg.astype(BF16)
    lo = (logg - hi.astype(F32)).astype(BF16)
    sums = (jnp.dot(mats, hi, preferred_element_type=F32)
            + jnp.dot(mats, lo, preferred_element_type=F32))
    b = sums[0:c]
    width = b.shape[1]
    e_b = jnp.exp(b)
    e_end = jnp.exp(jnp.minimum(b[c - 1:c] - b, 0.0))
    factors = []
    small = 0
    m = c // 2
    while m >= 1:
        if m >= SUBLANES:
            b3 = b.reshape(c // m, m, width)
            first = b3[:, 0:1, :]
            nxt = jnp.concatenate([first[1:], first[-1:]], axis=0)
            lq = jnp.minimum(b3 - first, 0.0).reshape(c, width)
            uk = jnp.minimum(nxt - b3, 0.0).reshape(c, width)
        else:
            lq = sums[(1 + 2 * small) * c:(2 + 2 * small) * c]
            uk = sums[(2 + 2 * small) * c:(3 + 2 * small) * c]
            small += 1
        factors.append((jnp.exp(lq), jnp.exp(uk)))
        m //= 2

    q = q_ref[...] * qscale
    k = k_ref[...]
    v = v_ref[...].astype(BF16)
    qb = q.astype(BF16)
    kb = k.astype(BF16)
    q_in = (q * e_b).astype(BF16)
    k_end = (k * e_end).astype(BF16)
    s_old = [s_ref[h] for h in hs]

    o = [jnp.dot(q_in[:, ks[h]], s_old[h].astype(BF16), preferred_element_type=F32) for h in hs]
    a = [mask_ref[nlev] * lax.dot_general(qb[:, ks[h]], kb[:, ks[h]], NT_DIMS, preferred_element_type=F32)
         for h in hs]
    for lv, (eq, ek) in enumerate(factors):
        qt = (q * eq).astype(BF16)
        kt = (k * ek).astype(BF16)
        a = [a[h] + mask_ref[lv] * lax.dot_general(qt[:, ks[h]], kt[:, ks[h]], NT_DIMS,
                                                   preferred_element_type=F32) for h in hs]
    o = [o[h] + jnp.dot(a[h].astype(BF16), v[:, vs[h]], preferred_element_type=F32) for h in hs]
    upd = [lax.dot_general(k_end[:, ks[h]], v[:, vs[h]], TN_DIMS, preferred_element_type=F32) for h in hs]

    decay_cols = jnp.broadcast_to(e_b[c - 1:c], (LANES, heads * dk)).T
    for h in hs:
        decay = jnp.concatenate([decay_cols[ks[h]]] * (dv // LANES), axis=1)
        s_ref[h] = decay * s_old[h] + upd[h]
        r = r_ref[:, vs[h]]
        o_ref[:, vs[h]] = (_rms(o[h], gn_ref[...]) * (r * jax.nn.sigmoid(r))).astype(o_ref.dtype)

    @pl.when(n == pl.num_programs(1) - 1)
    def _():
        sout_ref[0] = s_ref[...]


def _gla(z, zcols, gd, s0, batch0, wg, bg, gn, *, nb, seq, c, valid, out_dtype):
    _, heads, dk, dv = s0.shape
    rank = gd.shape[1]
    nc = seq // c
    wk, wv = heads * dk, heads * dv
    qc, kc, vc, rc = zcols
    assert qc % wk == 0 and kc % wk == 0 and vc % wv == 0 and rc % wv == 0
    mats, masks = _gla_tables(c)
    mats = jnp.asarray(mats, BF16)
    masks = jnp.asarray(masks, F32)

    def zspec(w, col0):
        return pl.BlockSpec((c, w), lambda b, n: (b * nc + n, col0 // w))

    return pl.pallas_call(
        functools.partial(_gla_kernel, c=c, valid=valid, qscale=dk ** -0.5),
        grid=(nb, nc),
        in_specs=[
            zspec(wk, qc), zspec(wk, kc), zspec(wv, vc),
            pl.BlockSpec((c, rank), lambda b, n: (b * nc + n, 0)),
            zspec(wv, rc),
            pl.BlockSpec((1, heads, dk, dv), lambda b, n: (batch0 + b, 0, 0, 0)),
            pl.BlockSpec((rank, wk), lambda b, n: (0, 0)),
            pl.BlockSpec((1, wk), lambda b, n: (0, 0)),
            pl.BlockSpec((1, dv), lambda b, n: (0, 0)),
            pl.BlockSpec(mats.shape, lambda b, n: (0, 0)),
            pl.BlockSpec(masks.shape, lambda b, n: (0, 0, 0)),
        ],
        out_specs=[
            pl.BlockSpec((c, wv), lambda b, n: (b * nc + n, 0)),
            pl.BlockSpec((1, heads, dk, dv), lambda b, n: (b, 0, 0, 0)),
        ],
        out_shape=[
            jax.ShapeDtypeStruct((nb * seq, wv), out_dtype),
            jax.ShapeDtypeStruct((nb, heads, dk, dv), F32),
        ],
        scratch_shapes=[pltpu.VMEM((heads, dk, dv), F32)],
        compiler_params=_params("parallel", "arbitrary"),
    )(z, z, z, gd, z, s0, wg, bg, gn, mats, masks)


def _oproj_kernel(oa_ref, og_ref, wa_ref, wg_ref, x_ref, y_ref):
    acc = jnp.dot(oa_ref[...], wa_ref[...], preferred_element_type=F32)
    acc = acc + jnp.dot(og_ref[...], wg_ref[...], preferred_element_type=F32)
    y_ref[...] = x_ref[...] + acc


def _oproj(oa, og, w, x, *, tm=1024, tn=1024):
    n, d = x.shape
    wa_rows = oa.shape[1]
    wg_rows = og.shape[1]
    assert wa_rows == wg_rows
    tm = _tile(n, tm)
    tn = _tile(d, tn)
    return pl.pallas_call(
        _oproj_kernel,
        grid=(n // tm, d // tn),
        in_specs=[
            pl.BlockSpec((tm, wa_rows), lambda i, j: (i, 0)),
            pl.BlockSpec((tm, wg_rows), lambda i, j: (i, 0)),
            pl.BlockSpec((wa_rows, tn), lambda i, j: (0, j)),
            pl.BlockSpec((wg_rows, tn), lambda i, j: (1, j)),
            pl.BlockSpec((tm, tn), lambda i, j: (i, j)),
        ],
        out_specs=pl.BlockSpec((tm, tn), lambda i, j: (i, j)),
        out_shape=jax.ShapeDtypeStruct((n, d), F32),
        compiler_params=_params("parallel", "arbitrary"),
    )(oa, og, w, w, x)


def _row(v):
    return v.reshape(1, -1).astype(F32)


def kernel(x_prompt, x_sample, cache_k, cache_v, state_gla, page_table, w_in, attn_q_norm, attn_k_norm, lam_q1, lam_k1, lam_q2, lam_k2, attn_out_norm, gla_w_gate_up, gla_b_gate, gla_out_norm, w_out, ffn1_norm, ffn1_w_gate, ffn1_w_up, ffn1_w_down, mix_norm, ffn2_norm, ffn2_w_gate, ffn2_w_up, ffn2_w_down, final_norm):
    nbp, seq, d = x_prompt.shape
    nbs, ld, _ = x_sample.shape
    depth, n_pool, page, heads_a, w2 = cache_k.shape
    hd = w2 // 2
    vd = cache_v.shape[-1]
    _, _, heads_g, dk, dv = state_gla.shape
    rank = gla_w_gate_up.shape[1]
    past = page_table.shape[1] * page
    rot = hd // 4
    wq = heads_a * w2
    wv = heads_a * vd
    wgq = heads_g * dk
    wgv = heads_g * dv
    c_q, c_k, c_v = 0, wq, 2 * wq
    c_rest = 2 * wq + wv
    w_rest = 2 * wgq + 2 * wgv
    c_gd = c_rest + w_rest
    zcols = (0, wgq, 2 * wgq, 2 * wgq + wgv)

    tab_p = _rope_tables(jnp.arange(seq), hd, rot)
    tab_s = _rope_tables(past + jnp.arange(nbs * ld) % ld, hd, rot)
    chunk = math.gcd(seq, GLA_CHUNK)
    ld_pad = -(-ld // 8) * 8

    xp = x_prompt.reshape(nbp * seq, d)
    xs = x_sample.reshape(nbs * ld, d)
    cache_k4 = cache_k.reshape(depth * n_pool, page, heads_a, w2)
    cache_v4 = cache_v.reshape(depth * n_pool, page, heads_a, vd)
    state4 = state_gla.reshape(depth * nbs, heads_g, dk, dv)
    zero_state = jnp.zeros((nbp, heads_g, dk, dv), F32)
    outs = [[] for _ in range(6)]
    for l in range(depth):
        li = 0.8 - 0.6 * math.exp(-0.3 * l)
        ffn1_w = (ffn1_w_gate[l], ffn1_w_up[l], ffn1_w_down[l])
        ffn2_w = (ffn2_w_gate[l], ffn2_w_up[l], ffn2_w_down[l])
        sample_casts = nbs * ld <= FFN_ROWS
        if not sample_casts:
            ffn1_w = tuple(w.astype(BF16) for w in ffn1_w)
            ffn2_w = tuple(w.astype(BF16) for w in ffn2_w)
        w_in_b = w_in[l].astype(BF16)
        w_gd = w_in_b[:, c_gd:]
        w_out_b = w_out[l].astype(BF16)
        lam_vecs = jnp.stack([lam_q1[l], lam_k1[l], lam_q2[l], lam_k2[l]]).astype(F32)
        gq, gk, gn_a = _row(attn_q_norm[l]), _row(attn_k_norm[l]), _row(attn_out_norm[l])
        wg = gla_w_gate_up[l].astype(BF16)
        bg, gn_g = _row(gla_b_gate[l]), _row(gla_out_norm[l])

        def ffn(x, norm, w, post, final, emit):
            res = _ffn(x, _row(norm), *w, _row(post), final=final, emit_bf16=emit)
            return res if emit else (res, w)

        def dense_in(x, tables, q_dtype, w, emit):
            x1, w = ffn(x, ffn1_norm[l], w, ffn1_norm[l], False, emit)
            h = _norm(x1, _row(mix_norm[l]))
            (qa,) = _qk_proj(h, w_in_b, c_q, wq, gq, tables, [q_dtype], rot=rot)
            ka, ka_b = _qk_proj(h, w_in_b, c_k, wq, gk, tables, [F32, BF16], rot=rot)
            va, va_b = _proj(h, w_in_b, c_v, wv, [F32, BF16])
            (z,) = _proj(h, w_in_b, c_rest, w_rest, [F32])
            (gd,) = _proj(h, w_gd, 0, rank, [F32])
            return (x1, qa, ka, ka_b, va, va_b, z, gd), w

        def dense_out(x1, oa, og, w, emit):
            x2 = _oproj(oa, og, w_out_b, x1)
            return ffn(x2, ffn2_norm[l], w, final_norm[l], True, emit)

        (x1, qa, ka, ka_b, va, va_b, z, gd), ffn1_w = dense_in(xs, tab_s, F32, ffn1_w, sample_casts)
        oa = _sample_attention(qa.reshape(nbs, ld, wq), ka.reshape(nbs, ld, wq), va.reshape(nbs, ld, wv),
                               cache_k4, cache_v4, page_table, lam_vecs, gn_a, pool0=l * n_pool, hd=hd, li=li)
        pad = lambda a: jnp.pad(a.reshape(nbs, ld, -1), ((0, 0), (0, ld_pad - ld), (0, 0))).reshape(nbs * ld_pad, -1)
        og, s_s = _gla(pad(z), zcols, pad(gd), state4, l * nbs, wg, bg, gn_g,
                       nb=nbs, seq=ld_pad, c=ld_pad, valid=ld, out_dtype=F32)
        og = og.reshape(nbs, ld_pad, wgv)[:, :ld].reshape(nbs * ld, wgv)
        xs, ffn2_w = dense_out(x1, oa.reshape(nbs * ld, wv).astype(BF16), og.astype(BF16), ffn2_w, sample_casts)
        outs[3].append(ka.reshape(nbs, ld, heads_a, w2))
        outs[4].append(va.reshape(nbs, ld, heads_a, vd))
        outs[5].append(s_s)

        (x1, qa, ka, ka_b, va, va_b, z, gd), _ = dense_in(xp, tab_p, BF16, ffn1_w, False)
        oa = _prompt_attention(qa, ka_b, va_b, lam_vecs, gn_a, nb=nbp, seq=seq, hd=hd, li=li)
        og, s_p = _gla(z, zcols, gd, zero_state, 0, wg, bg, gn_g,
                       nb=nbp, seq=seq, c=chunk, valid=chunk, out_dtype=BF16)
        xp, _ = dense_out(x1, oa, og, ffn2_w, False)
        outs[0].append(ka.reshape(nbp, seq, heads_a, w2))
        outs[1].append(va.reshape(nbp, seq, heads_a, vd))
        outs[2].append(s_p)

    return (xp.reshape(nbp, seq, d), xs.reshape(nbs, ld, d), jnp.stack(outs[0]), jnp.stack(outs[1]),
            jnp.stack(outs[2]), jnp.stack(outs[3]), jnp.stack(outs[4]), jnp.stack(outs[5]))
```
